```python
import jax
import jax.numpy as jnp
from jax import lax
import numpy as np

D_MODEL = 4096
BATCH = 8
SEQ = 2048
DEPTH = 4
DEC_BATCH = 16
DEC_SEQ = 32
PAST_LEN = 2048

CHUNK = 64
HEAD_DIM = 128
GROUP_W = D_MODEL // 4
MIX_W = 4 * GROUP_W
FOX_HEADS = GROUP_W // HEAD_DIM
Q_BLOCK = 128
FGATE_BIAS_INIT = 3.0
POOL_WINDOWS = (2, 4, 8, 16)
POOL_GROUP = GROUP_W // len(POOL_WINDOWS)
POOL_HIST = max(POOL_WINDOWS) - 1
CONV_WIDTH = 31
CONV_HIST = CONV_WIDTH - 1
GMLP_CHUNK = 128
GMLP_HEADS = GROUP_W // HEAD_DIM
N_MEM = 256
MEM_HEADS = 4
MEM_W = MEM_HEADS * HEAD_DIM
N_EXPERTS = 32
TOP_K = 4
D_FF = D_MODEL // 4
SWIGLU_LIMIT = 7.0
SWIGLU_ALPHA = 1.702
MOE_BLOCK = 128
DEEPNORM_ALPHA = (2 * DEPTH) ** 0.25
DEEPNORM_BETA = (8 * DEPTH) ** -0.25
LN_EPS = 1e-5
NEG_INF = -1e30
IN_SPLIT_POINTS = (GROUP_W, 2 * GROUP_W, 3 * GROUP_W, 3 * GROUP_W + FOX_HEADS, 4 * GROUP_W + FOX_HEADS,
                   6 * GROUP_W + FOX_HEADS, 7 * GROUP_W + FOX_HEADS)
IN_COLS = 8 * GROUP_W + FOX_HEADS

kernel_name = 'hybrid_streaming_encoder_step'


def layer_norm(x, g, b):
    xf = x.astype(jnp.float32)
    mu = jnp.mean(xf, axis=-1, keepdims=True)
    var = jnp.mean(jnp.square(xf - mu), axis=-1, keepdims=True)
    return ((xf - mu) * lax.rsqrt(var + LN_EPS) * g + b).astype(x.dtype)


def group_rms_norm(y, g):
    B, L, _ = y.shape
    yf = y.astype(jnp.float32).reshape(B, L, 4, GROUP_W)
    yf = yf * lax.rsqrt(jnp.mean(yf * yf, axis=-1, keepdims=True) + LN_EPS)
    return (yf.reshape(B, L, MIX_W) * g).astype(y.dtype)


def fox_attention(q, k, v, cum_q, cum_k, q_pos, k_pos):
    B, Lq, H, Dh = q.shape
    qb = Q_BLOCK if Lq % Q_BLOCK == 0 else Lq
    nb = Lq // qb
    scale = Dh ** -0.5
    ck = jnp.transpose(cum_k, (0, 2, 1))[:, :, None, :]

    def block(args):
        qi, ci, pi = args
        s = jnp.einsum('bqhd,bkhd->bhqk', qi, k, preferred_element_type=jnp.float32) * scale
        s = s + jnp.transpose(ci, (0, 2, 1))[..., None] - ck
        s = jnp.where(k_pos[None, :] <= pi[:, None], s, NEG_INF)
        w = jax.nn.softmax(s, axis=-1).astype(v.dtype)
        return jnp.einsum('bhqk,bkhd->bqhd', w, v)

    qs = jnp.transpose(q.reshape(B, nb, qb, H, Dh), (1, 0, 2, 3, 4))
    cs = jnp.transpose(cum_q.reshape(B, nb, qb, H), (1, 0, 2, 3))
    ps = q_pos.reshape(nb, qb)
    out = lax.map(block, (qs, cs, ps))
    return jnp.transpose(out, (1, 0, 2, 3, 4)).reshape(B, Lq, H * Dh)


def pool_mixer(xp, hist, pos0, w_pool, pool_scale):
    B, L, _ = xp.shape
    if hist is None:
        hist = jnp.zeros((B, POOL_HIST, GROUP_W), xp.dtype)
    ext = jnp.concatenate([hist, xp], axis=1)
    cs = jnp.cumsum(ext.astype(jnp.float32), axis=1)
    cs = jnp.concatenate([jnp.zeros((B, 1, GROUP_W), jnp.float32), cs], axis=1)
    pos = pos0 + jnp.arange(L)
    end = cs[:, POOL_HIST + 1:]
    groups = []
    for g, w in enumerate(POOL_WINDOWS):
        sl = slice(g * POOL_GROUP, (g + 1) * POOL_GROUP)
        start = cs[:, POOL_HIST + 1 - w:POOL_HIST + 1 - w + L, sl]
        cnt = jnp.minimum(w, pos + 1).astype(jnp.float32)[None, :, None]
        groups.append((end[..., sl] - start) / cnt)
    pooled = (jnp.concatenate(groups, axis=-1) - xp.astype(jnp.float32)).astype(xp.dtype)
    mixed = jnp.einsum('blgc,gcd->blgd', pooled.reshape(B, L, len(POOL_WINDOWS), POOL_GROUP), w_pool)
    return mixed.reshape(B, L, GROUP_W) * pool_scale, ext[:, -POOL_HIST:]


def conv_mixer(xc, hist, w_dw, b_dw, ln_g, ln_b, w_pw, b_pw):
    B, L, _ = xc.shape
    a, gte = jnp.split(xc, 2, axis=-1)
    glu = a * jax.nn.sigmoid(gte)
    if hist is None:
        hist = jnp.zeros((B, CONV_HIST, GROUP_W), glu.dtype)
    ext = jnp.concatenate([hist, glu], axis=1)
    y = lax.conv_general_dilated(ext, w_dw[:, None, :], window_strides=(1,), padding='VALID',
                                 dimension_numbers=('NWC', 'WIO', 'NWC'),
                                 feature_group_count=GROUP_W) + b_dw
    y = jax.nn.silu(layer_norm(y, ln_g, ln_b))
    return y @ w_pw + b_pw, ext[:, -CONV_HIST:]


def gmlp_mixer(u, vg, ln_g, ln_b, w_spatial, b_spatial):
    B, L, _ = u.shape
    lc = min(GMLP_CHUNK, L)
    nc = L // lc
    vn = layer_norm(vg, ln_g, ln_b).reshape(B, nc, lc, GMLP_HEADS, HEAD_DIM)
    causal = jnp.tril(jnp.ones((lc, lc), dtype=bool))
    ws = jnp.where(causal[None], w_spatial[:, :lc, :lc], 0)
    mix = jnp.einsum('hts,bnshc->bnthc', ws, vn) + jnp.transpose(b_spatial[:, :lc])[:, :, None]
    return u * mix.reshape(B, L, GROUP_W), vn.reshape(B, L, GMLP_HEADS, HEAD_DIM)


def token_mixer(h, p, fox_past, pool_hist, conv_hist):
    B, L, _ = h.shape
    z = h @ p['w_in']
    q, k, v, f, pz, cz, u, vg = jnp.split(z, IN_SPLIT_POINTS, axis=-1)
    q = q.reshape(B, L, FOX_HEADS, HEAD_DIM)
    k = k.reshape(B, L, FOX_HEADS, HEAD_DIM)
    v = v.reshape(B, L, FOX_HEADS, HEAD_DIM)
    logf = jax.nn.log_sigmoid((f + p['b_fgate']).astype(jnp.float32))
    if fox_past is None:
        k_all, v_all, logf_all = k, v, logf
    else:
        pk, pv, plf = fox_past
        k_all = jnp.concatenate([pk, k], axis=1)
        v_all = jnp.concatenate([pv, v], axis=1)
        logf_all = jnp.concatenate([plf.astype(jnp.float32), logf], axis=1)
    n_keys = k_all.shape[1]
    pos0 = n_keys - L
    cum = jnp.cumsum(logf_all, axis=1)
    y_a = fox_attention(q, k_all, v_all, cum[:, pos0:], cum, pos0 + jnp.arange(L), jnp.arange(n_keys))
    y_b, pool_state = pool_mixer(pz, pool_hist, pos0, p['w_pool'], p['pool_scale'])
    y_c, conv_state = conv_mixer(cz, conv_hist, p['w_dw'], p['b_dw'], p['conv_ln_g'], p['conv_ln_b'],
                                 p['w_pw'], p['b_pw'])
    y_d, gmlp_v = gmlp_mixer(u, vg, p['gmlp_ln_g'], p['gmlp_ln_b'], p['w_spatial'], p['b_spatial'])
    y = jnp.concatenate([y_a.astype(h.dtype), y_b, y_c, y_d], axis=-1)
    y = group_rms_norm(y, p['group_norm_g']) @ p['w_out']
    return y, (k, v, logf, pool_state, conv_state, gmlp_v)


def memory_attention(h, mem_k, mem_v, w_xq, w_xo):
    B, L, _ = h.shape
    q = (h @ w_xq).reshape(B, L, MEM_HEADS, HEAD_DIM)
    s = jnp.einsum('blhd,bmhd->bhlm', q, mem_k, preferred_element_type=jnp.float32) * HEAD_DIM ** -0.5
    w = jax.nn.softmax(s, axis=-1).astype(mem_v.dtype)
    o = jnp.einsum('bhlm,bmhd->blhd', w, mem_v).reshape(B, L, MEM_W)
    return o @ w_xo


def clamped_swiglu(hb):
    gate, up = jnp.split(hb, 2, axis=-1)
    gate = jnp.minimum(gate, SWIGLU_LIMIT)
    up = jnp.clip(up, -SWIGLU_LIMIT, SWIGLU_LIMIT)
    return gate * jax.nn.sigmoid(SWIGLU_ALPHA * gate) * (up + 1.0)


def moe_ffn(x, w_router, b_router, w_gate_up, b_gate_up, w_down, b_down):
    T, D = x.shape
    logits = jnp.matmul(x, w_router, preferred_element_type=jnp.float32) + b_router.astype(jnp.float32)
    top_val, top_idx = lax.top_k(logits, TOP_K)
    gate = jax.nn.softmax(top_val, axis=-1).astype(x.dtype)
    n_assign = T * TOP_K
    flat_e = top_idx.reshape(n_assign)
    order = jnp.argsort(flat_e)
    sorted_e = flat_e[order]
    sorted_tok = (order // TOP_K).astype(jnp.int32)
    sorted_gate = gate.reshape(n_assign)[order]
    counts = jnp.bincount(flat_e, length=N_EXPERTS)
    padded = (counts + MOE_BLOCK - 1) // MOE_BLOCK * MOE_BLOCK
    pad_end = jnp.cumsum(padded)
    pad_start = pad_end - padded
    grp_start = jnp.cumsum(counts) - counts
    dest = pad_start[sorted_e] + jnp.arange(n_assign) - grp_start[sorted_e]
    n_blocks = -(-(n_assign + N_EXPERTS * (MOE_BLOCK - 1)) // MOE_BLOCK)
    n_rows = n_blocks * MOE_BLOCK
    row_tok = jnp.full((n_rows,), T, jnp.int32).at[dest].set(sorted_tok)
    row_gate = jnp.zeros((n_rows,), x.dtype).at[dest].set(sorted_gate)
    blk_expert = jnp.minimum(jnp.searchsorted(pad_end, jnp.arange(n_blocks) * MOE_BLOCK, side='right'),
                             N_EXPERTS - 1)
    x_pad = jnp.concatenate([x, jnp.zeros((1, D), x.dtype)], axis=0)

    def block(acc, inp):
        tok, g, e = inp
        hb = x_pad[tok] @ w_gate_up[e] + b_gate_up[e]
        yb = (clamped_swiglu(hb) @ w_down[e] + b_down[e]) * g[:, None]
        return acc.at[tok].add(yb), None

    acc, _ = lax.scan(block, jnp.zeros((T + 1, D), x.dtype),
                      (row_tok.reshape(n_blocks, MOE_BLOCK), row_gate.reshape(n_blocks, MOE_BLOCK), blk_expert))
    return acc[:T]


def trunk_layer(x, p, mem_k, mem_v, fox_past, pool_hist, conv_hist):
    y, state = token_mixer(x, p, fox_past, pool_hist, conv_hist)
    h = layer_norm(DEEPNORM_ALPHA * x + y, p['ln1_g'], p['ln1_b'])
    h = layer_norm(DEEPNORM_ALPHA * h + memory_attention(h, mem_k, mem_v, p['w_xq'], p['w_xo']),
                   p['ln2_g'], p['ln2_b'])
    B, L, D = h.shape
    m = moe_ffn(h.reshape(B * L, D), p['w_router'], p['b_router'], p['w_gate_up'], p['b_gate_up'],
                p['w_down'], p['b_down']).reshape(B, L, D)
    return layer_norm(DEEPNORM_ALPHA * h + m, p['ln3_g'], p['ln3_b']), state


def setup_inputs(seed: int = 0) -> dict:
    key = jax.random.key(seed)
    ks = iter(jax.random.split(key, 64))
    f32 = jnp.float32

    def nrm(shape, scale=1.0):
        return jax.random.normal(next(ks), shape, f32) * scale

    def gain(shape):
        return 1.0 + nrm(shape, 0.1)

    def bias(shape):
        return nrm(shape, 0.02)

    return {
        'x_prompt': nrm((BATCH, SEQ, D_MODEL)),
        'x_sample': nrm((DEC_BATCH, DEC_SEQ, D_MODEL)),
        'mem_prompt': nrm((BATCH, N_MEM, D_MODEL)),
        'cache_fox_k': nrm((DEPTH, DEC_BATCH, PAST_LEN, FOX_HEADS, HEAD_DIM)),
        'cache_fox_v': nrm((DEPTH, DEC_BATCH, PAST_LEN, FOX_HEADS, HEAD_DIM)),
        'cache_fox_logf': jax.nn.log_sigmoid(FGATE_BIAS_INIT + nrm((DEPTH, DEC_BATCH, PAST_LEN, FOX_HEADS))),
        'cache_pool': nrm((DEPTH, DEC_BATCH, POOL_HIST, GROUP_W)),
        'cache_conv': nrm((DEPTH, DEC_BATCH, CONV_HIST, GROUP_W), 0.5),
        'cache_mem_k': nrm((DEPTH, DEC_BATCH, N_MEM, MEM_HEADS, HEAD_DIM)),
        'cache_mem_v': nrm((DEPTH, DEC_BATCH, N_MEM, MEM_HEADS, HEAD_DIM)),
        'emb_ln_g': gain((D_MODEL,)),
        'emb_ln_b': bias((D_MODEL,)),
        'w_in': nrm((DEPTH, D_MODEL, IN_COLS), D_MODEL ** -0.5),
        'b_fgate': FGATE_BIAS_INIT + nrm((DEPTH, FOX_HEADS), 0.1),
        'w_pool': nrm((DEPTH, len(POOL_WINDOWS), POOL_GROUP, POOL_GROUP), POOL_GROUP ** -0.5),
        'pool_scale': gain((DEPTH, GROUP_W)),
        'w_dw': nrm((DEPTH, CONV_WIDTH, GROUP_W), CONV_WIDTH ** -0.5),
        'b_dw': bias((DEPTH, GROUP_W)),
        'conv_ln_g': gain((DEPTH, GROUP_W)),
        'conv_ln_b': bias((DEPTH, GROUP_W)),
        'w_pw': nrm((DEPTH, GROUP_W, GROUP_W), GROUP_W ** -0.5),
        'b_pw': bias((DEPTH, GROUP_W)),
        'gmlp_ln_g': gain((DEPTH, GROUP_W)),
        'gmlp_ln_b': bias((DEPTH, GROUP_W)),
        'w_spatial': nrm((DEPTH, GMLP_HEADS, GMLP_CHUNK, GMLP_CHUNK), GMLP_CHUNK ** -0.5),
        'b_spatial': gain((DEPTH, GMLP_HEADS, GMLP_CHUNK)),
        'group_norm_g': gain((DEPTH, MIX_W)),
        'w_out': nrm((DEPTH, MIX_W, D_MODEL), MIX_W ** -0.5 * DEEPNORM_BETA),
        'ln1_g': gain((DEPTH, D_MODEL)),
        'ln1_b': bias((DEPTH, D_MODEL)),
        'w_xq': nrm((DEPTH, D_MODEL, MEM_W), D_MODEL ** -0.5),
        'w_xk': nrm((DEPTH, D_MODEL, MEM_W), D_MODEL ** -0.5),
        'w_xv': nrm((DEPTH, D_MODEL, MEM_W), D_MODEL ** -0.5),
        'w_xo': nrm((DEPTH, MEM_W, D_MODEL), MEM_W ** -0.5 * DEEPNORM_BETA),
        'ln2_g': gain((DEPTH, D_MODEL)),
        'ln2_b': bias((DEPTH, D_MODEL)),
        'w_router': nrm((DEPTH, D_MODEL, N_EXPERTS), D_MODEL ** -0.5),
        'b_router': nrm((DEPTH, N_EXPERTS), 0.01),
        'w_gate_up': nrm((DEPTH, N_EXPERTS, D_MODEL, 2 * D_FF), D_MODEL ** -0.5),
        'b_gate_up': bias((DEPTH, N_EXPERTS, 2 * D_FF)),
        'w_down': nrm((DEPTH, N_EXPERTS, D_FF, D_MODEL), D_FF ** -0.5 * DEEPNORM_BETA),
        'b_down': bias((DEPTH, N_EXPERTS, D_MODEL)),
        'ln3_g': gain((DEPTH, D_MODEL)),
        'ln3_b': bias((DEPTH, D_MODEL)),
    }


def reference(x_prompt, x_sample, mem_prompt, cache_fox_k, cache_fox_v, cache_fox_logf, cache_pool, cache_conv,
              cache_mem_k, cache_mem_v, emb_ln_g, emb_ln_b, w_in, b_fgate, w_pool, pool_scale, w_dw, b_dw,
              conv_ln_g, conv_ln_b, w_pw, b_pw, gmlp_ln_g, gmlp_ln_b, w_spatial, b_spatial, group_norm_g, w_out,
              ln1_g, ln1_b, w_xq, w_xk, w_xv, w_xo, ln2_g, ln2_b, w_router, b_router, w_gate_up, b_gate_up,
              w_down, b_down, ln3_g, ln3_b):
    xp = layer_norm(x_prompt, emb_ln_g, emb_ln_b)
    xs = layer_norm(x_sample, emb_ln_g, emb_ln_b)
    bp = xp.shape[0]
    fk_p, fv_p, fl_p, pool_p, conv_p, mk_p, mv_p = [], [], [], [], [], [], []
    fk_s, fv_s, fl_s, pool_s, conv_s, gv_s = [], [], [], [], [], []
    for l in range(DEPTH):
        p = {
            'w_in': w_in[l], 'b_fgate': b_fgate[l], 'w_pool': w_pool[l], 'pool_scale': pool_scale[l],
            'w_dw': w_dw[l], 'b_dw': b_dw[l], 'conv_ln_g': conv_ln_g[l], 'conv_ln_b': conv_ln_b[l],
            'w_pw': w_pw[l], 'b_pw': b_pw[l], 'gmlp_ln_g': gmlp_ln_g[l], 'gmlp_ln_b': gmlp_ln_b[l],
            'w_spatial': w_spatial[l], 'b_spatial': b_spatial[l], 'group_norm_g': group_norm_g[l],
            'w_out': w_out[l], 'ln1_g': ln1_g[l], 'ln1_b': ln1_b[l], 'w_xq': w_xq[l], 'w_xo': w_xo[l],
            'ln2_g': ln2_g[l], 'ln2_b': ln2_b[l], 'w_router': w_router[l], 'b_router': b_router[l],
            'w_gate_up': w_gate_up[l], 'b_gate_up': b_gate_up[l], 'w_down': w_down[l], 'b_down': b_down[l],
            'ln3_g': ln3_g[l], 'ln3_b': ln3_b[l],
        }
        mk = (mem_prompt @ w_xk[l]).reshape(bp, N_MEM, MEM_HEADS, HEAD_DIM)
        mv = (mem_prompt @ w_xv[l]).reshape(bp, N_MEM, MEM_HEADS, HEAD_DIM)
        xp, (k1, v1, lf1, ps1, cs1, _) = trunk_layer(xp, p, mk, mv, None, None, None)
        fk_p.append(k1); fv_p.append(v1); fl_p.append(lf1); pool_p.append(ps1); conv_p.append(cs1)
        mk_p.append(mk); mv_p.append(mv)
        xs, (k2, v2, lf2, ps2, cs2, gv2) = trunk_layer(
            xs, p, cache_mem_k[l], cache_mem_v[l], (cache_fox_k[l], cache_fox_v[l], cache_fox_logf[l]),
            cache_pool[l], cache_conv[l])
        fk_s.append(k2); fv_s.append(v2); fl_s.append(lf2); pool_s.append(ps2); conv_s.append(cs2)
        gv_s.append(gv2)
    return (xp, xs,
            jnp.stack(fk_p), jnp.stack(fv_p), jnp.stack(fl_p), jnp.stack(pool_p), jnp.stack(conv_p),
            jnp.stack(mk_p), jnp.stack(mv_p),
            jnp.stack(fk_s), jnp.stack(fv_s), jnp.stack(fl_s), jnp.stack(pool_s), jnp.stack(conv_s),
            jnp.stack(gv_s))
```

```python
import functools
import math

import jax
import jax.numpy as jnp
from jax import lax
from jax.experimental import pallas as pl
from jax.experimental.pallas import tpu as pltpu

F32 = jnp.float32
BF16 = jnp.bfloat16
I32 = jnp.int32

HEAD_DIM = 128
LANES = 128
LN_EPS = 1e-5
NEG_INF = -1e30
POOL_WINDOWS = (2, 4, 8, 16)
TOP_K = 4
SWIGLU_LIMIT = 7.0
SWIGLU_ALPHA = 1.702
VMEM_CAP = 60 * 1024 * 1024


def _pick(n, target, mult=8):
    if n <= target:
        return n
    t = target - target % mult
    while t >= mult:
        if n % t == 0:
            return t
        t -= mult
    return n


def _params(est_bytes, semantics):
    limit = int(min(VMEM_CAP, max(16 * 1024 * 1024, est_bytes * 5 // 4 + (2 << 20))))
    return pltpu.CompilerParams(dimension_semantics=semantics, vmem_limit_bytes=limit)


def _layer_norm(x, g, b):
    mu = jnp.mean(x, axis=-1, keepdims=True)
    xc = x - mu
    var = jnp.mean(xc * xc, axis=-1, keepdims=True)
    return xc * lax.rsqrt(var + LN_EPS) * g + b


def _rms_gain(y, g):
    return y * lax.rsqrt(jnp.mean(y * y, axis=-1, keepdims=True) + LN_EPS) * g


def _log_sigmoid(x):
    return -(jnp.maximum(-x, 0.0) + jnp.log1p(jnp.exp(-jnp.abs(x))))


def _add_ln_kernel(*refs, alpha, has_res):
    if has_res:
        x_ref, y_ref, g_ref, b_ref, o32_ref, o16_ref = refs
        x = alpha * x_ref[...] + y_ref[...].astype(F32)
    else:
        x_ref, g_ref, b_ref, o32_ref, o16_ref = refs
        x = x_ref[...]
    out = _layer_norm(x, g_ref[...], b_ref[...])
    o32_ref[...] = out
    o16_ref[...] = out.astype(BF16)


def _add_ln(x, y, g3, b3, layer, alpha):
    m, d = x.shape
    tm = _pick(m, 256)
    row = pl.BlockSpec((tm, d), lambda i: (i, 0))
    par = pl.BlockSpec((None, 1, d), lambda i: (layer, 0, 0))
    ins = [x] + ([y] if y is not None else []) + [g3, b3]
    specs = [row] + ([row] if y is not None else []) + [par, par]
    est = tm * d * (4 * (2 if y is not None else 1) + 4 + 2) * 2 + tm * d * 8
    return pl.pallas_call(
        functools.partial(_add_ln_kernel, alpha=alpha, has_res=y is not None),
        out_shape=(jax.ShapeDtypeStruct((m, d), F32), jax.ShapeDtypeStruct((m, d), BF16)),
        grid=(m // tm,),
        in_specs=specs,
        out_specs=(row, row),
        compiler_params=_params(est, ("arbitrary",)),
        name="add_ln",
    )(*ins)


def _mm_kernel(*refs, ks, cast_w, has_bias, epilogue, n_out):
    n_a = len(ks)
    a_refs = refs[:n_a]
    w_ref = refs[n_a]
    pos = n_a + 1
    bias_ref = None
    if has_bias:
        bias_ref = refs[pos]
        pos += 1
    o_refs = refs[pos:pos + n_out]
    if cast_w:
        wb_ref = refs[pos + n_out]

        @pl.when(pl.program_id(1) == 0)
        def _():
            wb_ref[...] = w_ref[...].astype(BF16)
        w = wb_ref
    else:
        w = w_ref
    acc = None
    off = 0
    for a_ref, k in zip(a_refs, ks):
        part = jnp.dot(a_ref[...], w[off:off + k, :], preferred_element_type=F32)
        acc = part if acc is None else acc + part
        off += k
    if has_bias:
        acc = acc + bias_ref[...]
    if epilogue == "log_sigmoid":
        acc = _log_sigmoid(acc)
    for o_ref in o_refs:
        o_ref[...] = acc.astype(o_ref.dtype)


def _dense_mm(a_list, w3, layer, col_off, n, out_dtypes, *, bias3=None, epilogue=None,
              tm_target=512, tn_target=512):
    m = a_list[0].shape[0]
    ks = tuple(a.shape[1] for a in a_list)
    k = sum(ks)
    tm = _pick(m, tm_target, 16)
    tn = _pick(math.gcd(n, col_off), tn_target, LANES)
    assert col_off % tn == 0 and n % tn == 0
    off_blk = col_off // tn
    cast_w = w3.dtype != BF16
    a_specs = [pl.BlockSpec((tm, kg), lambda j, i: (i, 0)) for kg in ks]
    w_spec = pl.BlockSpec((None, k, tn), lambda j, i: (layer, 0, off_blk + j))
    ins = list(a_list) + [w3]
    specs = a_specs + [w_spec]
    if bias3 is not None:
        ins.append(bias3)
        specs.append(pl.BlockSpec((None, 1, tn), lambda j, i: (layer, 0, off_blk + j)))
    out_spec = pl.BlockSpec((tm, tn), lambda j, i: (i, j))
    out_shape = tuple(jax.ShapeDtypeStruct((m, n), dt) for dt in out_dtypes)
    scratch = [pltpu.VMEM((k, tn), BF16)] if cast_w else []
    est = (2 * tm * k * 2 + 2 * k * tn * w3.dtype.itemsize + (k * tn * 2 if cast_w else 0)
           + sum(2 * tm * tn * jnp.dtype(dt).itemsize for dt in out_dtypes) + 2 * tm * tn * 4)
    outs = pl.pallas_call(
        functools.partial(_mm_kernel, ks=ks, cast_w=cast_w, has_bias=bias3 is not None,
                          epilogue=epilogue, n_out=len(out_dtypes)),
        out_shape=out_shape,
        grid=(n // tn, m // tm),
        in_specs=specs,
        out_specs=tuple(out_spec for _ in out_dtypes),
        scratch_shapes=scratch,
        compiler_params=_params(est, ("arbitrary", "arbitrary")),
        name="dense_mm",
    )(*ins)
    return outs


def _cumsum_kernel(x_ref, tri_ref, o_ref, *, blk, n_blk):
    tri = tri_ref[...]
    carry = jnp.zeros((1, x_ref.shape[1]), F32)
    for r in range(n_blk):
        x = x_ref[r * blk:(r + 1) * blk, :]
        hi = x.astype(BF16)
        r1 = x - hi.astype(F32)
        mid = r1.astype(BF16)
        lo = (r1 - mid.astype(F32)).astype(BF16)
        c = (jnp.dot(tri, hi, preferred_element_type=F32)
             + jnp.dot(tri, mid, preferred_element_type=F32)
             + jnp.dot(tri, lo, preferred_element_type=F32)) + carry
        o_ref[r * blk:(r + 1) * blk, :] = c
        carry = c[blk - 1:blk, :]


def _cumsum_time(x):
    lp, c = x.shape
    blk = _pick(lp, 256)
    tri = jnp.tri(blk, dtype=BF16)
    full = lambda shp: pl.BlockSpec(shp, lambda i: (0,) * len(shp))
    return pl.pallas_call(
        functools.partial(_cumsum_kernel, blk=blk, n_blk=lp // blk),
        out_shape=jax.ShapeDtypeStruct((lp, c), F32),
        grid=(1,),
        in_specs=[full((lp, c)), full((blk, blk))],
        out_specs=full((lp, c)),
        compiler_params=_params(4 * lp * LANES * 4 * 2, ("arbitrary",)),
        name="cumsum_time",
    )(x, tri)


def _softmax_step(q, k, v, cq, ck, carry, scale, mask):
    m, l, acc = carry
    s = lax.dot_general(q, k, (((1,), (1,)), ((), ())), preferred_element_type=F32) * scale
    s = s + cq - ck
    if mask is not None:
        s = jnp.where(mask, s, NEG_INF)
    m_new = jnp.maximum(m, jnp.max(s, axis=-1, keepdims=True))
    p = jnp.exp(s - m_new)
    a = jnp.exp(m - m_new)
    l_new = a * l + jnp.sum(p, axis=-1, keepdims=True)
    acc_new = a * acc + jnp.dot(p.astype(BF16), v, preferred_element_type=F32)
    return m_new, l_new, acc_new


def _fox_prompt_kernel(q_ref, k_ref, v_ref, cq_ref, ck_ref, gn_ref, y_ref, acc_ref, *, tq, heads, scale):
    qi = pl.program_id(1)
    row = lax.broadcasted_iota(I32, (tq, tq), 0)
    col = lax.broadcasted_iota(I32, (tq, tq), 1)
    causal = col <= row
    for h in range(heads):
        sl = slice(h * HEAD_DIM, (h + 1) * HEAD_DIM)
        q = q_ref[:, sl]
        cq = cq_ref[:, h:h + 1]

        def step(j, carry, mask, sl=sl, q=q, cq=cq, h=h):
            ks = pl.multiple_of(j * tq, tq)
            k = k_ref[pl.ds(ks, tq), sl]
            v = v_ref[pl.ds(ks, tq), sl]
            ck = ck_ref[h, pl.ds(j, 1), :]
            return _softmax_step(q, k, v, cq, ck, carry, scale, mask)

        init = (jnp.full((tq, 1), NEG_INF, F32), jnp.zeros((tq, 1), F32),
                jnp.zeros((tq, HEAD_DIM), F32))
        carry = lax.fori_loop(0, qi, lambda j, c: step(j, c, None), init)
        _, l, acc = step(qi, carry, causal)
        acc_ref[:, sl] = acc / l
    y_ref[...] = _rms_gain(acc_ref[...], gn_ref[...]).astype(y_ref.dtype)


def _fox_prompt(q16, kv16, cq, ck4, gn3, layer, n_seq, seq, tq):
    gw = q16.shape[1]
    heads = gw // HEAD_DIM
    nq = seq // tq
    est = 2 * (tq * gw * 2 + 2 * seq * gw * 2 + tq * LANES * 4 + heads * nq * tq * 4 + tq * gw * 2) \
        + tq * gw * 4 + 6 * tq * tq * 4
    return pl.pallas_call(
        functools.partial(_fox_prompt_kernel, tq=tq, heads=heads, scale=HEAD_DIM ** -0.5),
        out_shape=jax.ShapeDtypeStruct((n_seq * seq, gw), BF16),
        grid=(n_seq, nq),
        in_specs=[
            pl.BlockSpec((tq, gw), lambda b, i: (b * nq + i, 0)),
            pl.BlockSpec((seq, gw), lambda b, i: (b, 0)),
            pl.BlockSpec((seq, gw), lambda b, i: (b, 1)),
            pl.BlockSpec((tq, heads), lambda b, i: (b * nq + i, 0)),
            pl.BlockSpec((None, heads, nq, tq), lambda b, i: (b, 0, 0, 0)),
            pl.BlockSpec((None, 1, gw), lambda b, i: (layer, 0, 0)),
        ],
        out_specs=pl.BlockSpec((tq, gw), lambda b, i: (b * nq + i, 0)),
        scratch_shapes=[pltpu.VMEM((tq, gw), F32)],
        compiler_params=_params(est, ("arbitrary", "arbitrary")),
        name="fox_prompt",
    )(q16, kv16, kv16, cq, ck4, gn3)


def _fox_sample_kernel(q_ref, kc_ref, vc_ref, kn_ref, vn_ref, cq_ref, ckc_ref, ckn_ref, gn_ref, y_ref,
                       m_ref, l_ref, acc_ref, *, ls, heads, scale, n_kc):
    kc = pl.program_id(1)

    @pl.when(kc == 0)
    def _():
        m_ref[...] = jnp.full(m_ref.shape, NEG_INF, F32)
        l_ref[...] = jnp.zeros(l_ref.shape, F32)
        acc_ref[...] = jnp.zeros(acc_ref.shape, F32)

    def update(h, k, v, ck, mask):
        sl = slice(h * HEAD_DIM, (h + 1) * HEAD_DIM)
        carry = (m_ref[:, h:h + 1], l_ref[:, h:h + 1], acc_ref[:, sl])
        m, l, acc = _softmax_step(q_ref[:, sl], k, v, cq_ref[:, h:h + 1], ck, carry, scale, mask)
        m_ref[:, h:h + 1] = m
        l_ref[:, h:h + 1] = l
        acc_ref[:, sl] = acc

    for h in range(heads):
        sl = slice(h * HEAD_DIM, (h + 1) * HEAD_DIM)
        update(h, kc_ref[:, sl].astype(BF16), vc_ref[:, sl].astype(BF16), ckc_ref[h, pl.ds(kc, 1), :], None)

    @pl.when(kc == n_kc - 1)
    def _():
        row = lax.broadcasted_iota(I32, (ls, ls), 0)
        col = lax.broadcasted_iota(I32, (ls, ls), 1)
        for h in range(heads):
            sl = slice(h * HEAD_DIM, (h + 1) * HEAD_DIM)
            update(h, kn_ref[:, sl], vn_ref[:, sl], ckn_ref[h:h + 1, :], col <= row)
        for h in range(heads):
            sl = slice(h * HEAD_DIM, (h + 1) * HEAD_DIM)
            acc_ref[:, sl] = acc_ref[:, sl] / l_ref[:, h:h + 1]
        y_ref[...] = _rms_gain(acc_ref[...], gn_ref[...]).astype(y_ref.dtype)


def _fox_sample(q16, kv16, cache_k, cache_v, cq, ckc4, ckn, gn3, layer, row0, n_seq, ls):
    gw = q16.shape[1]
    heads = gw // HEAD_DIM
    past = cache_k.shape[2]
    tkc = _pick(past, 512)
    n_kc = past // tkc
    blk0 = row0 // ls
    est = 2 * (3 * ls * gw * 2 + 2 * tkc * gw * 4 + heads * past * 4 + ls * gw * 2) + 3 * ls * gw * 4 \
        + 2 * tkc * gw * 2 + 6 * ls * tkc * 4
    return pl.pallas_call(
        functools.partial(_fox_sample_kernel, ls=ls, heads=heads, scale=HEAD_DIM ** -0.5, n_kc=n_kc),
        out_shape=jax.ShapeDtypeStruct((n_seq * ls, gw), BF16),
        grid=(n_seq, n_kc),
        in_specs=[
            pl.BlockSpec((ls, gw), lambda b, c: (blk0 + b, 0)),
            pl.BlockSpec((None, None, tkc, gw), lambda b, c: (layer, b, c, 0)),
            pl.BlockSpec((None, None, tkc, gw), lambda b, c: (layer, b, c, 0)),
            pl.BlockSpec((ls, gw), lambda b, c: (blk0 + b, 0)),
            pl.BlockSpec((ls, gw), lambda b, c: (blk0 + b, 1)),
            pl.BlockSpec((ls, heads), lambda b, c: (blk0 + b, 0)),
            pl.BlockSpec((None, heads, n_kc, tkc), lambda b, c: (b, 0, 0, 0)),
            pl.BlockSpec((None, heads, ls), lambda b, c: (b, 0, 0)),
            pl.BlockSpec((None, 1, gw), lambda b, c: (layer, 0, 0)),
        ],
        out_specs=pl.BlockSpec((ls, gw), lambda b, c: (b, 0)),
        scratch_shapes=[pltpu.VMEM((ls, heads), F32), pltpu.VMEM((ls, heads), F32), pltpu.VMEM((ls, gw), F32)],
        compiler_params=_params(est, ("arbitrary", "arbitrary")),
        name="fox_sample",
    )(q16, cache_k, cache_v, kv16, kv16, cq, ckc4, ckn, gn3)


def _pool_kernel(*refs, tl, n_chunks, has_hist, pos0, hist_len):
    if has_hist:
        x_ref, hist_ref, wbd_ref, scale_ref, gn_ref, y_ref, state_ref, ext_ref = refs
    else:
        x_ref, wbd_ref, scale_ref, gn_ref, y_ref, state_ref, ext_ref = refs
    gw = x_ref.shape[1]
    pre = 16
    c = pl.program_id(1)

    @pl.when(c == 0)
    def _():
        ext_ref[0:pre, :] = jnp.zeros((pre, gw), F32)
        if has_hist:
            ext_ref[pre - hist_len:pre, :] = hist_ref[...]

    @pl.when(c > 0)
    def _():
        ext_ref[0:pre, :] = ext_ref[tl:tl + pre, :]

    x = x_ref[...].astype(F32)
    ext_ref[pre:pre + tl, :] = x
    pos = pos0 + c * tl + lax.broadcasted_iota(I32, (tl, 1), 0)
    pg = gw // len(POOL_WINDOWS)
    parts = []
    for g, w in enumerate(POOL_WINDOWS):
        cs = slice(g * pg, (g + 1) * pg)
        acc = x[:, cs]
        for j in range(1, w):
            acc = acc + ext_ref[pre - j:pre - j + tl, cs]
        cnt = jnp.minimum(w, pos + 1).astype(F32)
        parts.append(acc / cnt)
    pooled = jnp.concatenate(parts, axis=-1) - x
    mixed = jnp.dot(pooled.astype(BF16), wbd_ref[...], preferred_element_type=F32) * scale_ref[...]
    y_ref[...] = _rms_gain(mixed, gn_ref[...]).astype(y_ref.dtype)

    @pl.when(c == n_chunks - 1)
    def _():
        state_ref[...] = ext_ref[pre + tl - hist_len:pre + tl, :]


def _pool_mixer(z, col_blk, hist, wbd3, scale3, gn3, gn_blk, layer, row0, n_seq, seq, pos0):
    gw = wbd3.shape[-1]
    hist_len = max(POOL_WINDOWS) - 1
    tl = _pick(seq, 256)
    assert seq >= hist_len and tl >= 16 and row0 % tl == 0
    n_chunks = seq // tl
    blk0 = row0 // tl
    has_hist = hist is not None
    ins = [z] + ([hist] if has_hist else []) + [wbd3, scale3, gn3]
    specs = [pl.BlockSpec((tl, gw), lambda b, c: (blk0 + b * n_chunks + c, col_blk))]
    if has_hist:
        specs.append(pl.BlockSpec((None, None, hist_len, gw), lambda b, c: (layer, b, 0, 0)))
    specs += [
        pl.BlockSpec((None, gw, gw), lambda b, c: (layer, 0, 0)),
        pl.BlockSpec((None, 1, gw), lambda b, c: (layer, 0, 0)),
        pl.BlockSpec((None, 1, gw), lambda b, c: (layer, 0, gn_blk)),
    ]
    est = 2 * (tl * gw * 4 + gw * gw * 2 + tl * gw * 2 + 16 * gw * 4) + (tl + 16) * gw * 4 + 6 * tl * gw * 4
    return pl.pallas_call(
        functools.partial(_pool_kernel, tl=tl, n_chunks=n_chunks, has_hist=has_hist, pos0=pos0,
                          hist_len=hist_len),
        out_shape=(jax.ShapeDtypeStruct((n_seq * seq, gw), BF16),
                   jax.ShapeDtypeStruct((n_seq, hist_len, gw), F32)),
        grid=(n_seq, n_chunks),
        in_specs=specs,
        out_specs=(pl.BlockSpec((tl, gw), lambda b, c: (b * n_chunks + c, 0)),
                   pl.BlockSpec((None, hist_len, gw), lambda b, c: (b, 0, 0))),
        scratch_shapes=[pltpu.VMEM((tl + 16, gw), F32)],
        compiler_params=_params(est, ("arbitrary", "arbitrary")),
        name="pool_mixer",
    )(*ins)


def _conv_kernel(*refs, tl, n_chunks, has_hist, width):
    if has_hist:
        (a_ref, g_ref, hist_ref, wdw_ref, bdw_ref, lng_ref, lnb_ref, wpw_ref, bpw_ref, gn_ref,
         y_ref, state_ref, ext_ref) = refs
    else:
        (a_ref, g_ref, wdw_ref, bdw_ref, lng_ref, lnb_ref, wpw_ref, bpw_ref, gn_ref,
         y_ref, state_ref, ext_ref) = refs
    gw = a_ref.shape[1]
    hist_len = width - 1
    pre = 32
    c = pl.program_id(1)

    @pl.when(c == 0)
    def _():
        ext_ref[0:pre, :] = jnp.zeros((pre, gw), F32)
        if has_hist:
            ext_ref[pre - hist_len:pre, :] = hist_ref[...]

    @pl.when(c > 0)
    def _():
        ext_ref[0:pre, :] = ext_ref[tl:tl + pre, :]

    glu = a_ref[...].astype(F32) * jax.nn.sigmoid(g_ref[...].astype(F32))
    ext_ref[pre:pre + tl, :] = glu
    acc = jnp.zeros((tl, gw), F32) + bdw_ref[...]
    base = pre - hist_len
    for j in range(width):
        acc = acc + ext_ref[base + j:base + j + tl, :] * wdw_ref[j:j + 1, :]
    y = _layer_norm(acc, lng_ref[...], lnb_ref[...])
    y = y * jax.nn.sigmoid(y)
    out = jnp.dot(y.astype(BF16), wpw_ref[...], preferred_element_type=F32) + bpw_ref[...]
    y_ref[...] = _rms_gain(out, gn_ref[...]).astype(y_ref.dtype)

    @pl.when(c == n_chunks - 1)
    def _():
        state_ref[...] = ext_ref[pre + tl - hist_len:pre + tl, :]


def _conv_mixer(z, col_blk, hist, wdw, bdw3, lng3, lnb3, wpw3, bpw3, gn3, gn_blk, layer, row0, n_seq, seq):
    gw = wpw3.shape[-1]
    width = wdw.shape[1]
    hist_len = width - 1
    tl = _pick(seq, 256)
    assert seq >= hist_len and tl >= 32 and row0 % tl == 0
    n_chunks = seq // tl
    blk0 = row0 // tl
    has_hist = hist is not None
    vec = lambda blk=0: pl.BlockSpec((None, 1, gw), lambda b, c: (layer, 0, blk))
    ins = [z, z] + ([hist] if has_hist else []) + [wdw, bdw3, lng3, lnb3, wpw3, bpw3, gn3]
    specs = [pl.BlockSpec((tl, gw), lambda b, c: (blk0 + b * n_chunks + c, col_blk)),
             pl.BlockSpec((tl, gw), lambda b, c: (blk0 + b * n_chunks + c, col_blk + 1))]
    if has_hist:
        specs.append(pl.BlockSpec((None, None, hist_len, gw), lambda b, c: (layer, b, 0, 0)))
    specs += [pl.BlockSpec((None, width, gw), lambda b, c: (layer, 0, 0)), vec(), vec(), vec(),
              pl.BlockSpec((None, gw, gw), lambda b, c: (layer, 0, 0)), vec(), vec(gn_blk)]
    est = 2 * (2 * tl * gw * 4 + gw * gw * 2 + tl * gw * 2 + 64 * gw * 4) + (tl + 32) * gw * 4 + 8 * tl * gw * 4
    return pl.pallas_call(
        functools.partial(_conv_kernel, tl=tl, n_chunks=n_chunks, has_hist=has_hist, width=width),
        out_shape=(jax.ShapeDtypeStruct((n_seq * seq, gw), BF16),
                   jax.ShapeDtypeStruct((n_seq, hist_len, gw), F32)),
        grid=(n_seq, n_chunks),
        in_specs=specs,
        out_specs=(pl.BlockSpec((tl, gw), lambda b, c: (b * n_chunks + c, 0)),
                   pl.BlockSpec((None, hist_len, gw), lambda b, c: (b, 0, 0))),
        scratch_shapes=[pltpu.VMEM((tl + 32, gw), F32)],
        compiler_params=_params(est, ("arbitrary", "arbitrary")),
        name="conv_mixer",
    )(*ins)


def _gmlp_kernel(*refs, tl, lc, heads, emit_vn):
    if emit_vn:
        u_ref, v_ref, lng_ref, lnb_ref, ws_ref, bias_ref, gn_ref, y_ref, vn_ref = refs
    else:
        u_ref, v_ref, lng_ref, lnb_ref, ws_ref, bias_ref, gn_ref, y_ref = refs
    vn = _layer_norm(v_ref[...].astype(F32), lng_ref[...], lnb_ref[...])
    if emit_vn:
        vn_ref[...] = vn
    vb = vn.astype(BF16)
    rows = []
    for ch in range(tl // lc):
        rs = slice(ch * lc, (ch + 1) * lc)
        parts = [jnp.dot(ws_ref[h], vb[rs, h * HEAD_DIM:(h + 1) * HEAD_DIM], preferred_element_type=F32)
                 for h in range(heads)]
        rows.append(jnp.concatenate(parts, axis=-1) + bias_ref[...])
    mix = rows[0] if len(rows) == 1 else jnp.concatenate(rows, axis=0)
    y = u_ref[...].astype(F32) * mix
    y_ref[...] = _rms_gain(y, gn_ref[...]).astype(y_ref.dtype)


def _gmlp_mixer(z, col_blk, lng3, lnb3, ws4, bias3, gn3, gn_blk, layer, row0, n_rows, lc, emit_vn):
    gw = lng3.shape[-1]
    heads = gw // HEAD_DIM
    tl = lc * max(1, min(256 // lc, n_rows // lc))
    while n_rows % tl or row0 % tl:
        tl -= lc
    blk0 = row0 // tl
    vec = lambda blk=0: pl.BlockSpec((None, 1, gw), lambda i: (layer, 0, blk))
    out_shape = [jax.ShapeDtypeStruct((n_rows, gw), BF16)]
    out_specs = [pl.BlockSpec((tl, gw), lambda i: (i, 0))]
    if emit_vn:
        out_shape.append(jax.ShapeDtypeStruct((n_rows, gw), F32))
        out_specs.append(pl.BlockSpec((tl, gw), lambda i: (i, 0)))
    est = 2 * (2 * tl * gw * 4 + heads * lc * lc * 2 + lc * gw * 4 + tl * gw * 6) + 6 * tl * gw * 4
    outs = pl.pallas_call(
        functools.partial(_gmlp_kernel, tl=tl, lc=lc, heads=heads, emit_vn=emit_vn),
        out_shape=tuple(out_shape),
        grid=(n_rows // tl,),
        in_specs=[pl.BlockSpec((tl, gw), lambda i: (blk0 + i, col_blk)),
                  pl.BlockSpec((tl, gw), lambda i: (blk0 + i, col_blk + 1)),
                  vec(), vec(),
                  pl.BlockSpec((None, heads, lc, lc), lambda i: (layer, 0, 0, 0)),
                  pl.BlockSpec((None, lc, gw), lambda i: (layer, 0, 0)),
                  vec(gn_blk)],
        out_specs=tuple(out_specs),
        compiler_params=_params(est, ("arbitrary",)),
        name="gmlp_mixer",
    )(z, z, lng3, lnb3, ws4, bias3, gn3)
    return outs


def _memattn_kernel(q_ref, k_ref, v_ref, o_ref, *, heads, scale):
    parts = []
    for h in range(heads):
        sl = slice(h * HEAD_DIM, (h + 1) * HEAD_DIM)
        k = k_ref[:, sl].astype(BF16)
        v = v_ref[:, sl].astype(BF16)
        s = lax.dot_general(q_ref[:, sl], k, (((1,), (1,)), ((), ())), preferred_element_type=F32) * scale
        e = jnp.exp(s - jnp.max(s, axis=-1, keepdims=True))
        p = e / jnp.sum(e, axis=-1, keepdims=True)
        parts.append(jnp.dot(p.astype(BF16), v, preferred_element_type=F32))
    o_ref[...] = jnp.concatenate(parts, axis=-1).astype(o_ref.dtype)


def _mem_attention(q16, mem_k, mem_v, n_mem, row0, n_rows, rows_per_seq, kv_blk0):
    mw = q16.shape[1]
    heads = mw // HEAD_DIM
    tm = _pick(rows_per_seq, 256)
    per = rows_per_seq // tm
    blk0 = row0 // tm
    assert row0 % tm == 0
    est = 2 * (tm * mw * 2 + 2 * n_mem * mw * 4 + tm * mw * 2) + 8 * tm * n_mem * 4
    return pl.pallas_call(
        functools.partial(_memattn_kernel, heads=heads, scale=HEAD_DIM ** -0.5),
        out_shape=jax.ShapeDtypeStruct((n_rows, mw), BF16),
        grid=(n_rows // tm,),
        in_specs=[pl.BlockSpec((tm, mw), lambda i: (blk0 + i, 0)),
                  pl.BlockSpec((n_mem, mw), lambda i: (kv_blk0 + i // per, 0)),
                  pl.BlockSpec((n_mem, mw), lambda i: (kv_blk0 + i // per, 0))],
        out_specs=pl.BlockSpec((tm, mw), lambda i: (i, 0)),
        compiler_params=_params(est, ("arbitrary",)),
        name="mem_attention",
    )(q16, mem_k, mem_v)


def _router_kernel(x_ref, w_ref, b_ref, tril_ref, idx_ref, gate_ref, rank_ref, cnt_ref, carry_ref, *,
                   n_experts):
    @pl.when(pl.program_id(0) == 0)
    def _():
        carry_ref[...] = jnp.zeros(carry_ref.shape, F32)

    tm = x_ref.shape[0]
    logits = jnp.dot(x_ref[...], w_ref[...], preferred_element_type=F32) + b_ref[...]
    lane = lax.broadcasted_iota(I32, (tm, LANES), 1)
    lane_f = lane.astype(F32)
    work = jnp.where(lane < n_experts, logits, -jnp.inf)
    vals, sels = [], []
    idx_out = jnp.zeros((tm, LANES), I32)
    for k in range(TOP_K):
        m = jnp.max(work, axis=-1, keepdims=True)
        idx = jnp.min(jnp.where(work == m, lane_f, float(LANES)), axis=-1, keepdims=True)
        sel = lane_f == idx
        vals.append(m)
        sels.append(sel)
        idx_out = jnp.where(lane == k, idx.astype(I32), idx_out)
        work = jnp.where(sel, -jnp.inf, work)
    exps = [jnp.exp(v - vals[0]) for v in vals]
    denom = exps[0]
    for e in exps[1:]:
        denom = denom + e
    gate_out = jnp.zeros((tm, LANES), F32)
    for k in range(TOP_K):
        gate_out = jnp.where(lane == k, exps[k] / denom, gate_out)
    base = carry_ref[...]
    rank_out = jnp.zeros((tm, LANES), I32)
    tril = tril_ref[...]
    for k in range(TOP_K):
        onehot = jnp.where(sels[k], 1.0, 0.0)
        before = jnp.dot(tril, onehot.astype(BF16), preferred_element_type=F32) + base
        rank = jnp.sum(onehot * before, axis=-1, keepdims=True)
        rank_out = jnp.where(lane == k, rank.astype(I32), rank_out)
        base = base + jnp.sum(onehot, axis=0, keepdims=True)
    carry_ref[...] = base
    idx_ref[...] = idx_out
    gate_ref[...] = gate_out
    rank_ref[...] = rank_out
    cnt_ref[...] = base.astype(I32)


def _router(x16, w3, b3, layer, n_experts):
    m, d = x16.shape
    tm = _pick(m, 256, 16)
    tril = jnp.tri(tm, k=-1, dtype=BF16)
    row = pl.BlockSpec((tm, LANES), lambda i: (i, 0))
    est = 2 * (tm * d * 2 + d * LANES * 2 + tm * tm * 2 + 3 * tm * LANES * 4) + 16 * tm * LANES * 4
    return pl.pallas_call(
        functools.partial(_router_kernel, n_experts=n_experts),
        out_shape=(jax.ShapeDtypeStruct((m, LANES), I32), jax.ShapeDtypeStruct((m, LANES), F32),
                   jax.ShapeDtypeStruct((m, LANES), I32), jax.ShapeDtypeStruct((1, LANES), I32)),
        grid=(m // tm,),
        in_specs=[pl.BlockSpec((tm, d), lambda i: (i, 0)),
                  pl.BlockSpec((None, d, LANES), lambda i: (layer, 0, 0)),
                  pl.BlockSpec((None, 1, LANES), lambda i: (layer, 0, 0)),
                  pl.BlockSpec((tm, tm), lambda i: (0, 0))],
        out_specs=(row, row, row, pl.BlockSpec((1, LANES), lambda i: (0, 0))),
        scratch_shapes=[pltpu.VMEM((1, LANES), F32)],
        compiler_params=_params(est, ("arbitrary",)),
        name="router",
    )(x16, w3, b3, tril)


def _gather_rows(src_hbm, idx_smem, slot, buf, sem, n, unroll=8):
    def body(j, _):
        t = idx_smem[slot, j]
        pltpu.make_async_copy(src_hbm.at[pl.ds(t, 1)], buf.at[slot, pl.ds(j, 1)], sem.at[slot]).start()
        return 0
    lax.fori_loop(0, n, body, 0, unroll=unroll)


def _wait_rows(src_hbm, slot, buf, sem, n, unroll=8):
    def body(j, _):
        pltpu.make_async_copy(src_hbm.at[pl.ds(0, 1)], buf.at[slot, pl.ds(j, 1)], sem.at[slot]).wait()
        return 0
    lax.fori_loop(0, n, body, 0, unroll=unroll)


def _dispatch_kernel(tok_ref, x_hbm, o_ref, idx_smem, buf, isem, gsem, *, tmb, n_blk):
    rb = pl.program_id(0)
    slot = rb % 2
    nxt = 1 - slot

    def idx_copy(blk, s):
        return pltpu.make_async_copy(tok_ref.at[blk], idx_smem.at[s], isem.at[s])

    @pl.when(rb == 0)
    def _():
        idx_copy(0, 0).start()
        idx_copy(0, 0).wait()
        _gather_rows(x_hbm, idx_smem, 0, buf, gsem, tmb)
        if n_blk > 1:
            idx_copy(1, 1).start()

    @pl.when(rb + 1 < n_blk)
    def _():
        idx_copy(rb + 1, nxt).wait()
        _gather_rows(x_hbm, idx_smem, nxt, buf, gsem, tmb)

    @pl.when(rb + 2 < n_blk)
    def _():
        idx_copy(rb + 2, slot).start()

    _wait_rows(x_hbm, slot, buf, gsem, tmb)
    o_ref[...] = buf[slot].astype(o_ref.dtype)


def _dispatch(row_tok2, x32, tmb):
    n_blk = row_tok2.shape[0]
    d = x32.shape[1]
    est = 2 * tmb * d * 4 + 2 * tmb * d * 2 + row_tok2.size * 4 * 2 + tmb * d * 4
    return pl.pallas_call(
        functools.partial(_dispatch_kernel, tmb=tmb, n_blk=n_blk),
        out_shape=jax.ShapeDtypeStruct((n_blk * tmb, d), BF16),
        grid=(n_blk,),
        in_specs=[pl.BlockSpec((n_blk, tmb), lambda i: (0, 0)),
                  pl.BlockSpec(memory_space=pl.ANY)],
        out_specs=pl.BlockSpec((tmb, d), lambda i: (i, 0)),
        scratch_shapes=[pltpu.SMEM((2, tmb), I32), pltpu.VMEM((2, tmb, d), F32),
                        pltpu.SemaphoreType.DMA((2,)), pltpu.SemaphoreType.DMA((2,))],
        compiler_params=_params(est, ("arbitrary",)),
        name="moe_dispatch",
    )(row_tok2, x32)


def _expert_changed(meta_ref, rb):
    prev = meta_ref[1 + jnp.maximum(rb - 1, 0)]
    return jnp.logical_or(rb == 0, meta_ref[1 + rb] != prev)


def _expert_up_kernel(meta_ref, x_ref, wg_ref, wu_ref, bg_ref, bu_ref, h_ref, wgb_ref, wub_ref):
    rb = pl.program_id(1)
    live = rb < meta_ref[0]

    @pl.when(jnp.logical_and(live, _expert_changed(meta_ref, rb)))
    def _():
        wgb_ref[...] = wg_ref[...].astype(BF16)
        wub_ref[...] = wu_ref[...].astype(BF16)

    @pl.when(live)
    def _():
        x = x_ref[...]
        gate = jnp.dot(x, wgb_ref[...], preferred_element_type=F32) + bg_ref[...]
        up = jnp.dot(x, wub_ref[...], preferred_element_type=F32) + bu_ref[...]
        gate = jnp.minimum(gate, SWIGLU_LIMIT)
        up = jnp.clip(up, -SWIGLU_LIMIT, SWIGLU_LIMIT)
        h_ref[...] = (gate * jax.nn.sigmoid(SWIGLU_ALPHA * gate) * (up + 1.0)).astype(h_ref.dtype)

    @pl.when(jnp.logical_not(live))
    def _():
        h_ref[...] = jnp.zeros(h_ref.shape, h_ref.dtype)


def _expert_up(meta, xs, w_gate_up, b_gate_up4, layer, tmb):
    n_rows, d = xs.shape
    n_blk = n_rows // tmb
    ff = w_gate_up.shape[-1] // 2
    tf = _pick(ff, 512, LANES)
    nf = ff // tf
    live_blk = lambda rb, m: jnp.minimum(rb, m[0] - 1)
    w_spec = lambda off: pl.BlockSpec((None, None, d, tf), lambda f, rb, m: (layer, m[1 + rb], 0, off + f))
    b_spec = lambda off: pl.BlockSpec((None, None, 1, tf), lambda f, rb, m: (layer, m[1 + rb], 0, off + f))
    est = 2 * (tmb * d * 2 + 2 * d * tf * 4 + tmb * tf * 2) + 2 * d * tf * 2 + 6 * tmb * tf * 4
    return pl.pallas_call(
        _expert_up_kernel,
        out_shape=jax.ShapeDtypeStruct((n_rows, ff), BF16),
        grid_spec=pltpu.PrefetchScalarGridSpec(
            num_scalar_prefetch=1,
            grid=(nf, n_blk),
            in_specs=[pl.BlockSpec((tmb, d), lambda f, rb, m: (live_blk(rb, m), 0)),
                      w_spec(0), w_spec(nf), b_spec(0), b_spec(nf)],
            out_specs=pl.BlockSpec((tmb, tf), lambda f, rb, m: (rb, f)),
            scratch_shapes=[pltpu.VMEM((d, tf), BF16), pltpu.VMEM((d, tf), BF16)]),
        compiler_params=_params(est, ("arbitrary", "arbitrary")),
        name="expert_up",
    )(meta, xs, w_gate_up, w_gate_up, b_gate_up4, b_gate_up4)


def _expert_down_kernel(meta_ref, h_ref, w_ref, b_ref, y_ref, wb_ref):
    rb = pl.program_id(1)
    live = rb < meta_ref[0]

    @pl.when(jnp.logical_and(live, _expert_changed(meta_ref, rb)))
    def _():
        wb_ref[...] = w_ref[...].astype(BF16)

    @pl.when(live)
    def _():
        y_ref[...] = jnp.dot(h_ref[...], wb_ref[...], preferred_element_type=F32) + b_ref[...]

    @pl.when(jnp.logical_not(live))
    def _():
        y_ref[...] = jnp.zeros(y_ref.shape, y_ref.dtype)


def _expert_down(meta, hs, w_down, b_down4, layer, tmb):
    n_rows, ff = hs.shape
    n_blk = n_rows // tmb
    d = w_down.shape[-1]
    tn = _pick(d, 2048, LANES)
    live_blk = lambda rb, m: jnp.minimum(rb, m[0] - 1)
    est = 2 * (tmb * ff * 2 + ff * tn * 4 + tmb * tn * 4) + ff * tn * 2 + 2 * tmb * tn * 4
    return pl.pallas_call(
        _expert_down_kernel,
        out_shape=jax.ShapeDtypeStruct((n_rows, d), F32),
        grid_spec=pltpu.PrefetchScalarGridSpec(
            num_scalar_prefetch=1,
            grid=(d // tn, n_blk),
            in_specs=[pl.BlockSpec((tmb, ff), lambda c, rb, m: (live_blk(rb, m), 0)),
                      pl.BlockSpec((None, None, ff, tn), lambda c, rb, m: (layer, m[1 + rb], 0, c)),
                      pl.BlockSpec((None, None, 1, tn), lambda c, rb, m: (layer, m[1 + rb], 0, c))],
            out_specs=pl.BlockSpec((tmb, tn), lambda c, rb, m: (rb, c)),
            scratch_shapes=[pltpu.VMEM((ff, tn), BF16)]),
        compiler_params=_params(est, ("arbitrary", "arbitrary")),
        name="expert_down",
    )(meta, hs, w_down, b_down4)


def _combine_kernel(dest_ref, y_hbm, gate_ref, h_ref, g_ref, b_ref, o32_ref, o16_ref,
                    idx_smem, buf, isem, gsem, *, tmc, n_blk, alpha):
    i = pl.program_id(0)
    slot = i % 2
    nxt = 1 - slot
    n = TOP_K * tmc

    def idx_copy(blk, s):
        return pltpu.make_async_copy(dest_ref.at[blk], idx_smem.at[s], isem.at[s])

    @pl.when(i == 0)
    def _():
        idx_copy(0, 0).start()
        idx_copy(0, 0).wait()
        _gather_rows(y_hbm, idx_smem, 0, buf, gsem, n)
        if n_blk > 1:
            idx_copy(1, 1).start()

    @pl.when(i + 1 < n_blk)
    def _():
        idx_copy(i + 1, nxt).wait()
        _gather_rows(y_hbm, idx_smem, nxt, buf, gsem, n)

    @pl.when(i + 2 < n_blk)
    def _():
        idx_copy(i + 2, slot).start()

    _wait_rows(y_hbm, slot, buf, gsem, n)
    gates = gate_ref[...]
    mix = None
    for k in range(TOP_K):
        part = gates[:, k:k + 1] * buf[slot, k * tmc:(k + 1) * tmc, :]
        mix = part if mix is None else mix + part
    out = _layer_norm(alpha * h_ref[...] + mix, g_ref[...], b_ref[...])
    o32_ref[...] = out
    o16_ref[...] = out.astype(BF16)


def _combine_ln(dest2, y, gates, h32, g3, b3, layer, alpha, tmc):
    m, d = h32.shape
    n_blk = m // tmc
    n = TOP_K * tmc
    row = pl.BlockSpec((tmc, d), lambda i: (i, 0))
    par = pl.BlockSpec((None, 1, d), lambda i: (layer, 0, 0))
    est = 2 * n * d * 4 + 2 * (tmc * d * 10 + tmc * LANES * 4) + dest2.size * 4 * 2 + 4 * tmc * d * 4
    return pl.pallas_call(
        functools.partial(_combine_kernel, tmc=tmc, n_blk=n_blk, alpha=alpha),
        out_shape=(jax.ShapeDtypeStruct((m, d), F32), jax.ShapeDtypeStruct((m, d), BF16)),
        grid=(n_blk,),
        in_specs=[pl.BlockSpec((n_blk, n), lambda i: (0, 0)),
                  pl.BlockSpec(memory_space=pl.ANY),
                  pl.BlockSpec((tmc, LANES), lambda i: (i, 0)),
                  row, par, par],
        out_specs=(row, row),
        scratch_shapes=[pltpu.SMEM((2, n), I32), pltpu.VMEM((2, n, d), F32),
                        pltpu.SemaphoreType.DMA((2,)), pltpu.SemaphoreType.DMA((2,))],
        compiler_params=_params(est, ("arbitrary",)),
        name="moe_combine_ln",
    )(dest2, y, gates, h32, g3, b3)


def _route_tables(idx, rank, counts, tmb, n_blk, tmc):
    t = idx.shape[0]
    n_exp = counts.shape[0]
    padded = (counts + tmb - 1) // tmb * tmb
    pad_end = jnp.cumsum(padded)
    pad_start = pad_end - padded
    onehot = idx[..., None] == jnp.arange(n_exp, dtype=I32)
    dest = rank + jnp.sum(jnp.where(onehot, pad_start, 0), axis=-1)
    n_used = pad_end[-1] // tmb
    blk = jnp.arange(n_blk, dtype=I32)
    blk_e = jnp.minimum(jnp.searchsorted(pad_end, blk * tmb, side="right"), n_exp - 1).astype(I32)
    blk_e = jnp.where(blk < n_used, blk_e, blk_e[jnp.maximum(n_used - 1, 0)])
    meta = jnp.concatenate([n_used[None].astype(I32), blk_e])
    tok = jnp.broadcast_to(jnp.arange(t, dtype=I32)[:, None], dest.shape)
    row_tok = jnp.zeros((n_blk * tmb,), I32).at[dest.reshape(-1)].set(tok.reshape(-1), unique_indices=True)
    dest2 = dest.reshape(t // tmc, tmc, TOP_K).transpose(0, 2, 1).reshape(t // tmc, TOP_K * tmc)
    return meta, row_tok.reshape(n_blk, tmb), dest2


def kernel(x_prompt, x_sample, mem_prompt, cache_fox_k, cache_fox_v, cache_fox_logf, cache_pool, cache_conv,
           cache_mem_k, cache_mem_v, emb_ln_g, emb_ln_b, w_in, b_fgate, w_pool, pool_scale, w_dw, b_dw,
           conv_ln_g, conv_ln_b, w_pw, b_pw, gmlp_ln_g, gmlp_ln_b, w_spatial, b_spatial, group_norm_g, w_out,
           ln1_g, ln1_b, w_xq, w_xk, w_xv, w_xo, ln2_g, ln2_b, w_router, b_router, w_gate_up, b_gate_up,
           w_down, b_down, ln3_g, ln3_b):
    bp, seq, d = x_prompt.shape
    bs, ls, _ = x_sample.shape
    depth = w_in.shape[0]
    gw = d // 4
    heads = gw // HEAD_DIM
    past = cache_fox_k.shape[2]
    n_mem = mem_prompt.shape[1]
    mw = w_xq.shape[-1]
    n_exp = w_router.shape[-1]
    tp, ts = bp * seq, bs * ls
    t = tp + ts
    alpha = float((2 * depth) ** 0.25)
    lc_p = min(w_spatial.shape[-1], seq)
    lc_s = min(w_spatial.shape[-1], ls)
    tq = _pick(seq, 256)
    nq = seq // tq

    vec3 = lambda a: a.reshape(depth, 1, a.shape[-1])
    rest0 = 3 * gw + heads
    w_rest16 = w_in[:, :, rest0:].astype(BF16)
    w_f = jnp.pad(w_in[:, :, 3 * gw:rest0], ((0, 0), (0, 0), (0, LANES - heads)))
    b_f = jnp.pad(b_fgate, ((0, 0), (0, LANES - heads))).reshape(depth, 1, LANES)
    n_win = len(POOL_WINDOWS)
    pg = gw // n_win
    wbd = jnp.zeros((depth, gw, gw), F32)
    for g in range(n_win):
        wbd = wbd.at[:, g * pg:(g + 1) * pg, g * pg:(g + 1) * pg].set(w_pool[:, g])
    wbd16 = wbd.astype(BF16)
    wpw16 = w_pw.astype(BF16)

    def spatial(lc):
        causal = jnp.tril(jnp.ones((lc, lc), dtype=bool))
        ws = jnp.where(causal[None, None], w_spatial[:, :, :lc, :lc], 0).astype(BF16)
        bias = jnp.repeat(jnp.transpose(b_spatial[:, :, :lc], (0, 2, 1)), HEAD_DIM, axis=-1)
        return ws, bias

    ws_p, bias_p = spatial(lc_p)
    ws_s, bias_s = spatial(lc_s)
    w_router16 = jnp.pad(w_router, ((0, 0), (0, 0), (0, LANES - n_exp))).astype(BF16)
    b_router3 = jnp.pad(b_router, ((0, 0), (0, LANES - n_exp))).reshape(depth, 1, LANES)
    b_gu4 = b_gate_up.reshape(depth, n_exp, 1, b_gate_up.shape[-1])
    b_dn4 = b_down.reshape(depth, n_exp, 1, d)
    gn3 = vec3(group_norm_g)
    cache_k4 = cache_fox_k.reshape(depth, bs, past, gw)
    cache_v4 = cache_fox_v.reshape(depth, bs, past, gw)
    cache_mk = cache_mem_k.reshape(depth * bs * n_mem, mw)
    cache_mv = cache_mem_v.reshape(depth * bs * n_mem, mw)
    mem16 = mem_prompt.reshape(bp * n_mem, d).astype(BF16)

    tmb = 256 if (t * TOP_K) % 256 == 0 else 64
    n_blk = -(-(t * TOP_K + n_exp * (tmb - 1)) // tmb)
    tmc = _pick(t, 64)

    x_all = jnp.concatenate([x_prompt.reshape(tp, d), x_sample.reshape(ts, d)], axis=0)
    x32, x16 = _add_ln(x_all, None, emb_ln_g.reshape(1, 1, d), emb_ln_b.reshape(1, 1, d), 0, alpha)

    outs = [[] for _ in range(13)]
    for l in range(depth):
        (q16,) = _dense_mm([x16], w_in, l, 0, gw, (BF16,))
        kv32, kv16 = _dense_mm([x16], w_in, l, gw, 2 * gw, (F32, BF16))
        (zr,) = _dense_mm([x16], w_rest16, l, 0, 5 * gw, (BF16,))
        (logf_pad,) = _dense_mm([x16], w_f, l, 0, LANES, (F32,), bias3=b_f, epilogue="log_sigmoid")
        logf = logf_pad[:, :heads]

        lf_p = logf[:tp].reshape(bp, seq, heads)
        lf_s = logf[tp:].reshape(bs, ls, heads)
        cum_p = _cumsum_time(jnp.transpose(lf_p, (1, 0, 2)).reshape(seq, bp * heads))
        cum_p = jnp.transpose(cum_p.reshape(seq, bp, heads), (1, 0, 2))
        lk = past + ls
        lkp = -(-lk // 256) * 256
        lf_all = jnp.concatenate([cache_fox_logf[l].astype(F32), lf_s], axis=1)
        lf_all = jnp.pad(jnp.transpose(lf_all, (1, 0, 2)).reshape(lk, bs * heads), ((0, lkp - lk), (0, 0)))
        cum_s = jnp.transpose(_cumsum_time(lf_all)[:lk].reshape(lk, bs, heads), (1, 0, 2))
        cq = jnp.concatenate([cum_p.reshape(tp, heads), cum_s[:, past:].reshape(ts, heads)], axis=0)
        ck_p = jnp.transpose(cum_p, (0, 2, 1)).reshape(bp, heads, nq, tq)
        ck_s = jnp.transpose(cum_s, (0, 2, 1))
        tkc = _pick(past, 512)
        ckc = ck_s[:, :, :past].reshape(bs, heads, past // tkc, tkc)
        ckn = ck_s[:, :, past:]

        ya_p = _fox_prompt(q16, kv16, cq, ck_p, gn3, l, bp, seq, tq)
        ya_s = _fox_sample(q16, kv16, cache_k4, cache_v4, cq, ckc, ckn, gn3, l, tp, bs, ls)
        yb_p, pool_p = _pool_mixer(zr, 0, None, wbd16, vec3(pool_scale), gn3, 1, l, 0, bp, seq, 0)
        yb_s, pool_s = _pool_mixer(zr, 0, cache_pool, wbd16, vec3(pool_scale), gn3, 1, l, tp, bs, ls, past)
        conv_args = (w_dw, vec3(b_dw), vec3(conv_ln_g), vec3(conv_ln_b), wpw16, vec3(b_pw), gn3, 2, l)
        yc_p, conv_p = _conv_mixer(zr, 1, None, *conv_args, 0, bp, seq)
        yc_s, conv_s = _conv_mixer(zr, 1, cache_conv, *conv_args, tp, bs, ls)
        (yd_p,) = _gmlp_mixer(zr, 3, vec3(gmlp_ln_g), vec3(gmlp_ln_b), ws_p, bias_p, gn3, 3, l, 0, tp, lc_p, False)
        yd_s, gv_s = _gmlp_mixer(zr, 3, vec3(gmlp_ln_g), vec3(gmlp_ln_b), ws_s, bias_s, gn3, 3, l, tp, ts, lc_s, True)
        ys = [jnp.concatenate([p, s], axis=0) for p, s in ((ya_p, ya_s), (yb_p, yb_s), (yc_p, yc_s), (yd_p, yd_s))]

        (mix,) = _dense_mm(ys, w_out, l, 0, d, (F32,))
        h1_32, h1_16 = _add_ln(x32, mix, vec3(ln1_g), vec3(ln1_b), l, alpha)
        (qm16,) = _dense_mm([h1_16], w_xq, l, 0, mw, (BF16,))
        (mk,) = _dense_mm([mem16], w_xk, l, 0, mw, (F32,))
        (mv,) = _dense_mm([mem16], w_xv, l, 0, mw, (F32,))
        om_p = _mem_attention(qm16, mk, mv, n_mem, 0, tp, seq, 0)
        om_s = _mem_attention(qm16, cache_mk, cache_mv, n_mem, tp, ts, ls, l * bs)
        (xo,) = _dense_mm([jnp.concatenate([om_p, om_s], axis=0)], w_xo, l, 0, d, (F32,))
        h2_32, h2_16 = _add_ln(h1_32, xo, vec3(ln2_g), vec3(ln2_b), l, alpha)

        idx_pad, gates, rank_pad, cnt_pad = _router(h2_16, w_router16, b_router3, l, n_exp)
        meta, row_tok2, dest2 = _route_tables(idx_pad[:, :TOP_K], rank_pad[:, :TOP_K], cnt_pad[0, :n_exp],
                                              tmb, n_blk, tmc)
        xs = _dispatch(row_tok2, h2_32, tmb)
        hs = _expert_up(meta, xs, w_gate_up, b_gu4, l, tmb)
        y = _expert_down(meta, hs, w_down, b_dn4, l, tmb)
        x32, x16 = _combine_ln(dest2, y, gates, h2_32, vec3(ln3_g), vec3(ln3_b), l, alpha, tmc)

        k32, v32 = kv32[:, :gw], kv32[:, gw:]
        per_layer = (
            k32[:tp].reshape(bp, seq, heads, HEAD_DIM), v32[:tp].reshape(bp, seq, heads, HEAD_DIM), lf_p,
            pool_p, conv_p, mk.reshape(bp, n_mem, mw // HEAD_DIM, HEAD_DIM),
            mv.reshape(bp, n_mem, mw // HEAD_DIM, HEAD_DIM),
            k32[tp:].reshape(bs, ls, heads, HEAD_DIM), v32[tp:].reshape(bs, ls, heads, HEAD_DIM), lf_s,
            pool_s, conv_s, gv_s.reshape(bs, ls, heads, HEAD_DIM))
        for acc, val in zip(outs, per_layer):
            acc.append(val)

    return (x32[:tp].reshape(bp, seq, d), x32[tp:].reshape(bs, ls, d)) + tuple(jnp.stack(o) for o in outs)
```

```python
import functools
import math

import jax
import jax.numpy as jnp
from jax import lax
from jax.experimental import pallas as pl
from jax.experimental.pallas import tpu as pltpu

F32 = jnp.float32
BF16 = jnp.bfloat16
I32 = jnp.int32

HEAD_DIM = 128
LANES = 128
LN_EPS = 1e-5
NEG_INF = -1e30
POOL_WINDOWS = (2, 4, 8, 16)
TOP_K = 4
SWIGLU_LIMIT = 7.0
SWIGLU_ALPHA = 1.702
VMEM_CAP = 60 * 1024 * 1024


def _pick(n, target, mult=8):
    if n <= target:
        return n
    t = target - target % mult
    while t >= mult:
        if n % t == 0:
            return t
        t -= mult
    return n


def _params(est_bytes, semantics):
    limit = int(min(VMEM_CAP, max(16 * 1024 * 1024, est_bytes * 5 // 4 + (2 << 20))))
    return pltpu.CompilerParams(dimension_semantics=semantics, vmem_limit_bytes=limit)


def _layer_norm(x, g, b):
    mu = jnp.mean(x, axis=-1, keepdims=True)
    xc = x - mu
    var = jnp.mean(xc * xc, axis=-1, keepdims=True)
    return xc * lax.rsqrt(var + LN_EPS) * g + b


def _rms_gain(y, g):
    return y * lax.rsqrt(jnp.mean(y * y, axis=-1, keepdims=True) + LN_EPS) * g


def _log_sigmoid(x):
    return -(jnp.maximum(-x, 0.0) + jnp.log1p(jnp.exp(-jnp.abs(x))))


def _add_ln_kernel(*refs, alpha, has_res, emit_rows):
    n_in = 4 if has_res else 3
    ins, outs = refs[:n_in], refs[n_in:]
    if has_res:
        x_ref, y_ref, g_ref, b_ref = ins
        x = alpha * x_ref[...] + y_ref[...].astype(F32)
    else:
        x_ref, g_ref, b_ref = ins
        x = x_ref[...]
    out = _layer_norm(x, g_ref[...], b_ref[...])
    outs[0][...] = out
    outs[1][...] = out.astype(BF16)
    if emit_rows:
        outs[2][...] = out.reshape(outs[2].shape)


def _add_ln(x, y, g3, b3, layer, alpha, emit_rows=False):
    m, d = x.shape
    tm = _pick(m, 256)
    row = pl.BlockSpec((tm, d), lambda i: (i, 0))
    par = pl.BlockSpec((None, 1, d), lambda i: (layer, 0, 0))
    ins = [x] + ([y] if y is not None else []) + [g3, b3]
    specs = [row] + ([row] if y is not None else []) + [par, par]
    out_shape = [jax.ShapeDtypeStruct((m, d), F32), jax.ShapeDtypeStruct((m, d), BF16)]
    out_specs = [row, row]
    if emit_rows:
        out_shape.append(jax.ShapeDtypeStruct((m, 1, d), F32))
        out_specs.append(pl.BlockSpec((tm, 1, d), lambda i: (i, 0, 0)))
    est = tm * d * (4 * (2 if y is not None else 1) + 4 + 2 + (4 if emit_rows else 0)) * 2 + tm * d * 8
    return pl.pallas_call(
        functools.partial(_add_ln_kernel, alpha=alpha, has_res=y is not None, emit_rows=emit_rows),
        out_shape=tuple(out_shape),
        grid=(m // tm,),
        in_specs=specs,
        out_specs=tuple(out_specs),
        compiler_params=_params(est, ("arbitrary",)),
        name="add_ln",
    )(*ins)


def _mm_kernel(*refs, ks, cast_w, has_bias, epilogue, n_out, n_head_blk):
    n_a = len(ks)
    n_groups = 2 if n_head_blk is not None else 1
    a_groups = [refs[g * n_a:(g + 1) * n_a] for g in range(n_groups)]
    w_ref = refs[n_groups * n_a]
    pos = n_groups * n_a + 1
    bias_ref = None
    if has_bias:
        bias_ref = refs[pos]
        pos += 1
    o_refs = refs[pos:pos + n_out]
    if cast_w:
        wb_ref = refs[pos + n_out]

        @pl.when(pl.program_id(1) == 0)
        def _():
            wb_ref[...] = w_ref[...].astype(BF16)
        w = wb_ref
    else:
        w = w_ref

    def run(a_refs):
        acc = None
        off = 0
        for a_ref, k in zip(a_refs, ks):
            part = jnp.dot(a_ref[...], w[off:off + k, :], preferred_element_type=F32)
            acc = part if acc is None else acc + part
            off += k
        if has_bias:
            acc = acc + bias_ref[...]
        if epilogue == "log_sigmoid":
            acc = _log_sigmoid(acc)
        for o_ref in o_refs:
            o_ref[...] = acc.astype(o_ref.dtype)

    if n_head_blk is None:
        run(a_groups[0])
    else:
        i = pl.program_id(1)
        pl.when(i < n_head_blk)(lambda: run(a_groups[0]))
        pl.when(i >= n_head_blk)(lambda: run(a_groups[1]))


def _dense_mm(a_list, w3, layer, col_off, n, out_dtypes, *, bias3=None, epilogue=None,
              tm_target=512, tn_target=512):
    split = isinstance(a_list[0], tuple)
    heads_ = [a[0] if split else a for a in a_list]
    ks = tuple(a.shape[1] for a in heads_)
    k = sum(ks)
    cast_w = w3.dtype != BF16
    if split:
        m_head, m_tail = a_list[0][0].shape[0], a_list[0][1].shape[0]
        m = m_head + m_tail
        tm = _pick(math.gcd(m_head, m_tail), tm_target, 16)
        n_head_blk = m_head // tm
        last_tail = m_tail // tm - 1
        a_specs = ([pl.BlockSpec((tm, kg), lambda j, i: (jnp.minimum(i, n_head_blk - 1), 0)) for kg in ks]
                   + [pl.BlockSpec((tm, kg), lambda j, i: (jnp.clip(i - n_head_blk, 0, last_tail), 0)) for kg in ks])
        ins = [a[0] for a in a_list] + [a[1] for a in a_list] + [w3]
    else:
        m = heads_[0].shape[0]
        tm = _pick(m, tm_target, 16)
        n_head_blk = None
        a_specs = [pl.BlockSpec((tm, kg), lambda j, i: (i, 0)) for kg in ks]
        ins = list(a_list) + [w3]
    tn = _pick(math.gcd(n, col_off), tn_target, LANES)
    assert col_off % tn == 0 and n % tn == 0
    off_blk = col_off // tn
    w_spec = pl.BlockSpec((None, k, tn), lambda j, i: (layer, 0, off_blk + j))
    specs = a_specs + [w_spec]
    if bias3 is not None:
        ins.append(bias3)
        specs.append(pl.BlockSpec((None, 1, tn), lambda j, i: (layer, 0, off_blk + j)))
    out_spec = pl.BlockSpec((tm, tn), lambda j, i: (i, j))
    out_shape = tuple(jax.ShapeDtypeStruct((m, n), dt) for dt in out_dtypes)
    scratch = [pltpu.VMEM((k, tn), BF16)] if cast_w else []
    est = ((4 if split else 2) * tm * k * 2 + 2 * k * tn * w3.dtype.itemsize + (k * tn * 2 if cast_w else 0)
           + sum(2 * tm * tn * jnp.dtype(dt).itemsize for dt in out_dtypes) + 2 * tm * tn * 4)
    outs = pl.pallas_call(
        functools.partial(_mm_kernel, ks=ks, cast_w=cast_w, has_bias=bias3 is not None,
                          epilogue=epilogue, n_out=len(out_dtypes), n_head_blk=n_head_blk),
        out_shape=out_shape,
        grid=(n // tn, m // tm),
        in_specs=specs,
        out_specs=tuple(out_spec for _ in out_dtypes),
        scratch_shapes=scratch,
        compiler_params=_params(est, ("arbitrary", "arbitrary")),
        name="dense_mm",
    )(*ins)
    return outs


def _split3(x):
    hi = x.astype(BF16)
    r1 = x - hi.astype(F32)
    mid = r1.astype(BF16)
    lo = (r1 - mid.astype(F32)).astype(BF16)
    return hi, mid, lo


def _cumsum_kernel(x_ref, tri_ref, o_ref, *split_refs, blk, n_blk, split_scale):
    tri = tri_ref[...]
    carry = jnp.zeros((1, x_ref.shape[1]), F32)
    for r in range(n_blk):
        rs = slice(r * blk, (r + 1) * blk)
        hi, mid, lo = _split3(x_ref[rs, :])
        c = (jnp.dot(tri, hi, preferred_element_type=F32)
             + jnp.dot(tri, mid, preferred_element_type=F32)
             + jnp.dot(tri, lo, preferred_element_type=F32)) + carry
        o_ref[rs, :] = c
        if split_refs:
            for ref, term in zip(split_refs, _split3(c * split_scale)):
                ref[rs, :] = term
        carry = c[blk - 1:blk, :]


def _cumsum_time(x, split_scale=None):
    lp, c = x.shape
    blk = _pick(lp, 256)
    tri = jnp.tri(blk, dtype=BF16)
    full = lambda shp: pl.BlockSpec(shp, lambda i: (0,) * len(shp))
    n_split = 3 if split_scale is not None else 0
    out_shape = (jax.ShapeDtypeStruct((lp, c), F32),) + (jax.ShapeDtypeStruct((lp, c), BF16),) * n_split
    return pl.pallas_call(
        functools.partial(_cumsum_kernel, blk=blk, n_blk=lp // blk, split_scale=split_scale),
        out_shape=out_shape,
        grid=(1,),
        in_specs=[full((lp, c)), full((blk, blk))],
        out_specs=tuple(full((lp, c)) for _ in out_shape),
        compiler_params=_params(6 * lp * LANES * 4 * 2, ("arbitrary",)),
        name="cumsum_time",
    )(x, tri)


def _softmax_step(q, k, v, cq, ck, carry, scale, mask):
    m, l, acc = carry
    s = lax.dot_general(q, k, (((1,), (1,)), ((), ())), preferred_element_type=F32) * scale
    s = s + cq - ck
    if mask is not None:
        s = jnp.where(mask, s, NEG_INF)
    m_new = jnp.maximum(m, jnp.max(s, axis=-1, keepdims=True))
    p = jnp.exp(s - m_new)
    a = jnp.exp(m - m_new)
    l_new = a * l + jnp.sum(p, axis=-1, keepdims=True)
    acc_new = a * acc + jnp.dot(p.astype(BF16), v, preferred_element_type=F32)
    return m_new, l_new, acc_new


def _fox_prompt_kernel(q_ref, aq_ref, k_ref, ak_ref, v_ref, gn_ref, y_ref, vt_ref, m_ref, l_ref, acct_ref,
                       acc_ref, s_ref, p_ref, *, tq, heads, c2):
    qi = pl.program_id(1)
    nk = k_ref.shape[0] // tq

    @pl.when(qi == 0)
    def _():
        for h in range(heads):
            sl = slice(h * HEAD_DIM, (h + 1) * HEAD_DIM)
            for j in range(nk):
                vt_ref[h, j] = v_ref[j * tq:(j + 1) * tq, sl].T

    m_ref[...] = jnp.full(m_ref.shape, NEG_INF, F32)
    l_ref[...] = jnp.zeros(l_ref.shape, F32)
    acct_ref[...] = jnp.zeros(acct_ref.shape, F32)

    def key_block(j, mask):
        ks = pl.multiple_of(j * tq, tq)
        for h in range(heads):
            sl = slice(h * HEAD_DIM, (h + 1) * HEAD_DIM)
            k2 = jnp.concatenate([k_ref[pl.ds(ks, tq), sl], ak_ref[pl.ds(ks, tq), sl]], axis=1)
            q2 = jnp.concatenate([q_ref[:, sl], aq_ref[:, sl]], axis=1)
            s_ref[h] = lax.dot_general(k2, q2, (((1,), (1,)), ((), ())), preferred_element_type=F32)
        for h in range(heads):
            s = s_ref[h] * c2
            if mask is not None:
                s = jnp.where(mask, s, NEG_INF)
            m_old = m_ref[h]
            m_new = jnp.maximum(m_old, jnp.max(s, axis=0, keepdims=True))
            p = jnp.exp2(s - m_new)
            a = jnp.exp2(m_old - m_new)
            l_ref[h] = a * l_ref[h] + jnp.sum(p, axis=0, keepdims=True)
            p_ref[h] = p.astype(BF16)
            acct_ref[h] = a * acct_ref[h]
            m_ref[h] = m_new
        for h in range(heads):
            acct_ref[h] += jnp.dot(vt_ref[h, j], p_ref[h], preferred_element_type=F32)

    def body(j, c):
        key_block(j, None)
        return c

    lax.fori_loop(0, qi, body, 0)
    key_idx = lax.broadcasted_iota(I32, (tq, tq), 0)
    qry_idx = lax.broadcasted_iota(I32, (tq, tq), 1)
    key_block(qi, key_idx <= qry_idx)
    for h in range(heads):
        sl = slice(h * HEAD_DIM, (h + 1) * HEAD_DIM)
        acc_ref[:, sl] = (acct_ref[h] / l_ref[h]).T
    y_ref[...] = _rms_gain(acc_ref[...], gn_ref[...]).astype(y_ref.dtype)


def _fox_prompt(q16, kv16, aug_q, aug_k, gn3, layer, n_seq, seq, tq):
    gw = q16.shape[1]
    heads = gw // HEAD_DIM
    nq = seq // tq
    qblk = pl.BlockSpec((tq, gw), lambda b, i: (b * nq + i, 0))
    est = 2 * (3 * tq * gw * 2 + 3 * seq * gw * 2) + seq * gw * 2 + tq * gw * 8 + 3 * heads * 8 * tq * 4 \
        + 8 * tq * tq * 4
    return pl.pallas_call(
        functools.partial(_fox_prompt_kernel, tq=tq, heads=heads, c2=HEAD_DIM ** -0.5 * math.log2(math.e)),
        out_shape=jax.ShapeDtypeStruct((n_seq * seq, gw), BF16),
        grid=(n_seq, nq),
        in_specs=[
            qblk, qblk,
            pl.BlockSpec((seq, gw), lambda b, i: (b, 0)),
            pl.BlockSpec((seq, gw), lambda b, i: (b, 0)),
            pl.BlockSpec((seq, gw), lambda b, i: (b, 1)),
            pl.BlockSpec((None, 1, gw), lambda b, i: (layer, 0, 0)),
        ],
        out_specs=qblk,
        scratch_shapes=[pltpu.VMEM((heads, nq, HEAD_DIM, tq), BF16),
                        pltpu.VMEM((heads, 1, tq), F32), pltpu.VMEM((heads, 1, tq), F32),
                        pltpu.VMEM((heads, HEAD_DIM, tq), F32), pltpu.VMEM((tq, gw), F32),
                        pltpu.VMEM((heads, tq, tq), F32), pltpu.VMEM((heads, tq, tq), BF16)],
        compiler_params=_params(est + heads * tq * tq * 6, ("arbitrary", "arbitrary")),
        name="fox_prompt",
    )(q16, aug_q, kv16, aug_k, kv16, gn3)


def _fox_sample_kernel(q_ref, kc_ref, vc_ref, kn_ref, vn_ref, cq_ref, ckc_ref, ckn_ref, gn_ref, y_ref,
                       m_ref, l_ref, acc_ref, *, ls, heads, scale, n_kc):
    kc = pl.program_id(1)

    @pl.when(kc == 0)
    def _():
        m_ref[...] = jnp.full(m_ref.shape, NEG_INF, F32)
        l_ref[...] = jnp.zeros(l_ref.shape, F32)
        acc_ref[...] = jnp.zeros(acc_ref.shape, F32)

    def update(h, k, v, ck, mask):
        sl = slice(h * HEAD_DIM, (h + 1) * HEAD_DIM)
        carry = (m_ref[:, h:h + 1], l_ref[:, h:h + 1], acc_ref[:, sl])
        m, l, acc = _softmax_step(q_ref[:, sl], k, v, cq_ref[:, h:h + 1], ck, carry, scale, mask)
        m_ref[:, h:h + 1] = m
        l_ref[:, h:h + 1] = l
        acc_ref[:, sl] = acc

    for h in range(heads):
        sl = slice(h * HEAD_DIM, (h + 1) * HEAD_DIM)
        update(h, kc_ref[:, sl].astype(BF16), vc_ref[:, sl].astype(BF16), ckc_ref[h, pl.ds(kc, 1), :], None)

    @pl.when(kc == n_kc - 1)
    def _():
        row = lax.broadcasted_iota(I32, (ls, ls), 0)
        col = lax.broadcasted_iota(I32, (ls, ls), 1)
        for h in range(heads):
            sl = slice(h * HEAD_DIM, (h + 1) * HEAD_DIM)
            update(h, kn_ref[:, sl], vn_ref[:, sl], ckn_ref[h:h + 1, :], col <= row)
        for h in range(heads):
            sl = slice(h * HEAD_DIM, (h + 1) * HEAD_DIM)
            acc_ref[:, sl] = acc_ref[:, sl] / l_ref[:, h:h + 1]
        y_ref[...] = _rms_gain(acc_ref[...], gn_ref[...]).astype(y_ref.dtype)


def _fox_sample(q16, kv16, cache_k, cache_v, cq, ckc4, ckn, gn3, layer, row0, n_seq, ls):
    gw = q16.shape[1]
    heads = gw // HEAD_DIM
    past = cache_k.shape[2]
    tkc = _pick(past, 512)
    n_kc = past // tkc
    blk0 = row0 // ls
    est = 2 * (3 * ls * gw * 2 + 2 * tkc * gw * 4 + heads * past * 4 + ls * gw * 2) + 3 * ls * gw * 4 \
        + 2 * tkc * gw * 2 + 6 * ls * tkc * 4
    return pl.pallas_call(
        functools.partial(_fox_sample_kernel, ls=ls, heads=heads, scale=HEAD_DIM ** -0.5, n_kc=n_kc),
        out_shape=jax.ShapeDtypeStruct((n_seq * ls, gw), BF16),
        grid=(n_seq, n_kc),
        in_specs=[
            pl.BlockSpec((ls, gw), lambda b, c: (blk0 + b, 0)),
            pl.BlockSpec((None, None, tkc, gw), lambda b, c: (layer, b, c, 0)),
            pl.BlockSpec((None, None, tkc, gw), lambda b, c: (layer, b, c, 0)),
            pl.BlockSpec((ls, gw), lambda b, c: (blk0 + b, 0)),
            pl.BlockSpec((ls, gw), lambda b, c: (blk0 + b, 1)),
            pl.BlockSpec((ls, heads), lambda b, c: (b, 0)),
            pl.BlockSpec((None, heads, n_kc, tkc), lambda b, c: (b, 0, 0, 0)),
            pl.BlockSpec((None, heads, ls), lambda b, c: (b, 0, 0)),
            pl.BlockSpec((None, 1, gw), lambda b, c: (layer, 0, 0)),
        ],
        out_specs=pl.BlockSpec((ls, gw), lambda b, c: (b, 0)),
        scratch_shapes=[pltpu.VMEM((ls, heads), F32), pltpu.VMEM((ls, heads), F32), pltpu.VMEM((ls, gw), F32)],
        compiler_params=_params(est, ("arbitrary", "arbitrary")),
        name="fox_sample",
    )(q16, cache_k, cache_v, kv16, kv16, cq, ckc4, ckn, gn3)


def _pool_kernel(*refs, tl, n_chunks, has_hist, pos0, hist_len):
    if has_hist:
        x_ref, hist_ref, wbd_ref, scale_ref, gn_ref, y_ref, state_ref, ext_ref = refs
    else:
        x_ref, wbd_ref, scale_ref, gn_ref, y_ref, state_ref, ext_ref = refs
    gw = x_ref.shape[1]
    pre = 16
    c = pl.program_id(1)

    @pl.when(c == 0)
    def _():
        ext_ref[0:pre, :] = jnp.zeros((pre, gw), F32)
        if has_hist:
            ext_ref[pre - hist_len:pre, :] = hist_ref[...]

    @pl.when(c > 0)
    def _():
        ext_ref[0:pre, :] = ext_ref[tl:tl + pre, :]

    x = x_ref[...].astype(F32)
    ext_ref[pre:pre + tl, :] = x
    pos = pos0 + c * tl + lax.broadcasted_iota(I32, (tl, 1), 0)
    pg = gw // len(POOL_WINDOWS)
    parts = []
    for g, w in enumerate(POOL_WINDOWS):
        cs = slice(g * pg, (g + 1) * pg)
        acc = x[:, cs]
        for j in range(1, w):
            acc = acc + ext_ref[pre - j:pre - j + tl, cs]
        cnt = jnp.minimum(w, pos + 1).astype(F32)
        parts.append(acc / cnt)
    pooled = jnp.concatenate(parts, axis=-1) - x
    mixed = jnp.dot(pooled.astype(BF16), wbd_ref[...], preferred_element_type=F32) * scale_ref[...]
    y_ref[...] = _rms_gain(mixed, gn_ref[...]).astype(y_ref.dtype)

    @pl.when(c == n_chunks - 1)
    def _():
        state_ref[...] = ext_ref[pre + tl - hist_len:pre + tl, :]


def _pool_mixer(z, col_blk, hist, wbd3, scale3, gn3, gn_blk, layer, row0, n_seq, seq, pos0):
    gw = wbd3.shape[-1]
    hist_len = max(POOL_WINDOWS) - 1
    tl = _pick(seq, 256)
    assert seq >= hist_len and tl >= 16 and row0 % tl == 0
    n_chunks = seq // tl
    blk0 = row0 // tl
    has_hist = hist is not None
    ins = [z] + ([hist] if has_hist else []) + [wbd3, scale3, gn3]
    specs = [pl.BlockSpec((tl, gw), lambda b, c: (blk0 + b * n_chunks + c, col_blk))]
    if has_hist:
        specs.append(pl.BlockSpec((None, None, hist_len, gw), lambda b, c: (layer, b, 0, 0)))
    specs += [
        pl.BlockSpec((None, gw, gw), lambda b, c: (layer, 0, 0)),
        pl.BlockSpec((None, 1, gw), lambda b, c: (layer, 0, 0)),
        pl.BlockSpec((None, 1, gw), lambda b, c: (layer, 0, gn_blk)),
    ]
    est = 2 * (tl * gw * 4 + gw * gw * 2 + tl * gw * 2 + 16 * gw * 4) + (tl + 16) * gw * 4 + 6 * tl * gw * 4
    return pl.pallas_call(
        functools.partial(_pool_kernel, tl=tl, n_chunks=n_chunks, has_hist=has_hist, pos0=pos0,
                          hist_len=hist_len),
        out_shape=(jax.ShapeDtypeStruct((n_seq * seq, gw), BF16),
                   jax.ShapeDtypeStruct((n_seq, hist_len, gw), F32)),
        grid=(n_seq, n_chunks),
        in_specs=specs,
        out_specs=(pl.BlockSpec((tl, gw), lambda b, c: (b * n_chunks + c, 0)),
                   pl.BlockSpec((None, hist_len, gw), lambda b, c: (b, 0, 0))),
        scratch_shapes=[pltpu.VMEM((tl + 16, gw), F32)],
        compiler_params=_params(est, ("arbitrary", "arbitrary")),
        name="pool_mixer",
    )(*ins)


def _conv_kernel(*refs, tl, n_chunks, has_hist, width):
    if has_hist:
        (a_ref, g_ref, hist_ref, wdw_ref, bdw_ref, lng_ref, lnb_ref, wpw_ref, bpw_ref, gn_ref,
         y_ref, state_ref, ext_ref) = refs
    else:
        (a_ref, g_ref, wdw_ref, bdw_ref, lng_ref, lnb_ref, wpw_ref, bpw_ref, gn_ref,
         y_ref, state_ref, ext_ref) = refs
    gw = a_ref.shape[1]
    hist_len = width - 1
    pre = 32
    c = pl.program_id(1)

    @pl.when(c == 0)
    def _():
        ext_ref[0:pre, :] = jnp.zeros((pre, gw), F32)
        if has_hist:
            ext_ref[pre - hist_len:pre, :] = hist_ref[...]

    @pl.when(c > 0)
    def _():
        ext_ref[0:pre, :] = ext_ref[tl:tl + pre, :]

    glu = a_ref[...].astype(F32) * jax.nn.sigmoid(g_ref[...].astype(F32))
    ext_ref[pre:pre + tl, :] = glu
    acc = jnp.zeros((tl, gw), F32) + bdw_ref[...]
    base = pre - hist_len
    for j in range(width):
        acc = acc + ext_ref[base + j:base + j + tl, :] * wdw_ref[j:j + 1, :]
    y = _layer_norm(acc, lng_ref[...], lnb_ref[...])
    y = y * jax.nn.sigmoid(y)
    out = jnp.dot(y.astype(BF16), wpw_ref[...], preferred_element_type=F32) + bpw_ref[...]
    y_ref[...] = _rms_gain(out, gn_ref[...]).astype(y_ref.dtype)

    @pl.when(c == n_chunks - 1)
    def _():
        state_ref[...] = ext_ref[pre + tl - hist_len:pre + tl, :]


def _conv_mixer(z, col_blk, hist, wdw, bdw3, lng3, lnb3, wpw3, bpw3, gn3, gn_blk, layer, row0, n_seq, seq):
    gw = wpw3.shape[-1]
    width = wdw.shape[1]
    hist_len = width - 1
    tl = _pick(seq, 256)
    assert seq >= hist_len and tl >= 32 and row0 % tl == 0
    n_chunks = seq // tl
    blk0 = row0 // tl
    has_hist = hist is not None
    vec = lambda blk=0: pl.BlockSpec((None, 1, gw), lambda b, c: (layer, 0, blk))
    ins = [z, z] + ([hist] if has_hist else []) + [wdw, bdw3, lng3, lnb3, wpw3, bpw3, gn3]
    specs = [pl.BlockSpec((tl, gw), lambda b, c: (blk0 + b * n_chunks + c, col_blk)),
             pl.BlockSpec((tl, gw), lambda b, c: (blk0 + b * n_chunks + c, col_blk + 1))]
    if has_hist:
        specs.append(pl.BlockSpec((None, None, hist_len, gw), lambda b, c: (layer, b, 0, 0)))
    specs += [pl.BlockSpec((None, width, gw), lambda b, c: (layer, 0, 0)), vec(), vec(), vec(),
              pl.BlockSpec((None, gw, gw), lambda b, c: (layer, 0, 0)), vec(), vec(gn_blk)]
    est = 2 * (2 * tl * gw * 4 + gw * gw * 2 + tl * gw * 2 + 64 * gw * 4) + (tl + 32) * gw * 4 + 8 * tl * gw * 4
    return pl.pallas_call(
        functools.partial(_conv_kernel, tl=tl, n_chunks=n_chunks, has_hist=has_hist, width=width),
        out_shape=(jax.ShapeDtypeStruct((n_seq * seq, gw), BF16),
                   jax.ShapeDtypeStruct((n_seq, hist_len, gw), F32)),
        grid=(n_seq, n_chunks),
        in_specs=specs,
        out_specs=(pl.BlockSpec((tl, gw), lambda b, c: (b * n_chunks + c, 0)),
                   pl.BlockSpec((None, hist_len, gw), lambda b, c: (b, 0, 0))),
        scratch_shapes=[pltpu.VMEM((tl + 32, gw), F32)],
        compiler_params=_params(est, ("arbitrary", "arbitrary")),
        name="conv_mixer",
    )(*ins)


def _gmlp_kernel(*refs, tl, lc, heads, emit_vn):
    if emit_vn:
        u_ref, v_ref, lng_ref, lnb_ref, ws_ref, bias_ref, gn_ref, y_ref, vn_ref = refs
    else:
        u_ref, v_ref, lng_ref, lnb_ref, ws_ref, bias_ref, gn_ref, y_ref = refs
    vn = _layer_norm(v_ref[...].astype(F32), lng_ref[...], lnb_ref[...])
    if emit_vn:
        vn_ref[...] = vn
    vb = vn.astype(BF16)
    rows = []
    for ch in range(tl // lc):
        rs = slice(ch * lc, (ch + 1) * lc)
        parts = [jnp.dot(ws_ref[h], vb[rs, h * HEAD_DIM:(h + 1) * HEAD_DIM], preferred_element_type=F32)
                 for h in range(heads)]
        rows.append(jnp.concatenate(parts, axis=-1) + bias_ref[...])
    mix = rows[0] if len(rows) == 1 else jnp.concatenate(rows, axis=0)
    y = u_ref[...].astype(F32) * mix
    y_ref[...] = _rms_gain(y, gn_ref[...]).astype(y_ref.dtype)


def _gmlp_mixer(z, col_blk, lng3, lnb3, ws4, bias3, gn3, gn_blk, layer, row0, n_rows, lc, emit_vn):
    gw = lng3.shape[-1]
    heads = gw // HEAD_DIM
    tl = lc * max(1, min(256 // lc, n_rows // lc))
    while n_rows % tl or row0 % tl:
        tl -= lc
    blk0 = row0 // tl
    vec = lambda blk=0: pl.BlockSpec((None, 1, gw), lambda i: (layer, 0, blk))
    out_shape = [jax.ShapeDtypeStruct((n_rows, gw), BF16)]
    out_specs = [pl.BlockSpec((tl, gw), lambda i: (i, 0))]
    if emit_vn:
        out_shape.append(jax.ShapeDtypeStruct((n_rows, gw), F32))
        out_specs.append(pl.BlockSpec((tl, gw), lambda i: (i, 0)))
    est = 2 * (2 * tl * gw * 4 + heads * lc * lc * 2 + lc * gw * 4 + tl * gw * 6) + 6 * tl * gw * 4
    outs = pl.pallas_call(
        functools.partial(_gmlp_kernel, tl=tl, lc=lc, heads=heads, emit_vn=emit_vn),
        out_shape=tuple(out_shape),
        grid=(n_rows // tl,),
        in_specs=[pl.BlockSpec((tl, gw), lambda i: (blk0 + i, col_blk)),
                  pl.BlockSpec((tl, gw), lambda i: (blk0 + i, col_blk + 1)),
                  vec(), vec(),
                  pl.BlockSpec((None, heads, lc, lc), lambda i: (layer, 0, 0, 0)),
                  pl.BlockSpec((None, lc, gw), lambda i: (layer, 0, 0)),
                  vec(gn_blk)],
        out_specs=tuple(out_specs),
        compiler_params=_params(est, ("arbitrary",)),
        name="gmlp_mixer",
    )(z, z, lng3, lnb3, ws4, bias3, gn3)
    return outs


def _memattn_kernel(q_ref, k_ref, v_ref, o_ref, *, heads, scale):
    parts = []
    for h in range(heads):
        sl = slice(h * HEAD_DIM, (h + 1) * HEAD_DIM)
        k = k_ref[:, sl].astype(BF16)
        v = v_ref[:, sl].astype(BF16)
        s = lax.dot_general(q_ref[:, sl], k, (((1,), (1,)), ((), ())), preferred_element_type=F32) * scale
        e = jnp.exp(s - jnp.max(s, axis=-1, keepdims=True))
        p = e / jnp.sum(e, axis=-1, keepdims=True)
        parts.append(jnp.dot(p.astype(BF16), v, preferred_element_type=F32))
    o_ref[...] = jnp.concatenate(parts, axis=-1).astype(o_ref.dtype)


def _mem_attention(q16, mem_k, mem_v, n_mem, row0, n_rows, rows_per_seq, kv_blk0):
    mw = q16.shape[1]
    heads = mw // HEAD_DIM
    tm = _pick(rows_per_seq, 256)
    per = rows_per_seq // tm
    blk0 = row0 // tm
    assert row0 % tm == 0
    est = 2 * (tm * mw * 2 + 2 * n_mem * mw * 4 + tm * mw * 2) + 8 * tm * n_mem * 4
    return pl.pallas_call(
        functools.partial(_memattn_kernel, heads=heads, scale=HEAD_DIM ** -0.5),
        out_shape=jax.ShapeDtypeStruct((n_rows, mw), BF16),
        grid=(n_rows // tm,),
        in_specs=[pl.BlockSpec((tm, mw), lambda i: (blk0 + i, 0)),
                  pl.BlockSpec((n_mem, mw), lambda i: (kv_blk0 + i // per, 0)),
                  pl.BlockSpec((n_mem, mw), lambda i: (kv_blk0 + i // per, 0))],
        out_specs=pl.BlockSpec((tm, mw), lambda i: (i, 0)),
        compiler_params=_params(est, ("arbitrary",)),
        name="mem_attention",
    )(q16, mem_k, mem_v)


def _router_kernel(x_ref, w_ref, b_ref, tril_ref, idx_ref, gate_ref, rank_ref, cnt_ref, carry_ref, *,
                   n_experts):
    @pl.when(pl.program_id(0) == 0)
    def _():
        carry_ref[...] = jnp.zeros(carry_ref.shape, F32)

    tm = x_ref.shape[0]
    logits = jnp.dot(x_ref[...], w_ref[...], preferred_element_type=F32) + b_ref[...]
    lane = lax.broadcasted_iota(I32, (tm, LANES), 1)
    lane_f = lane.astype(F32)
    work = jnp.where(lane < n_experts, logits, -jnp.inf)
    vals, sels = [], []
    idx_out = jnp.zeros((tm, LANES), I32)
    for k in range(TOP_K):
        m = jnp.max(work, axis=-1, keepdims=True)
        idx = jnp.min(jnp.where(work == m, lane_f, float(LANES)), axis=-1, keepdims=True)
        sel = lane_f == idx
        vals.append(m)
        sels.append(sel)
        idx_out = jnp.where(lane == k, idx.astype(I32), idx_out)
        work = jnp.where(sel, -jnp.inf, work)
    exps = [jnp.exp(v - vals[0]) for v in vals]
    denom = exps[0]
    for e in exps[1:]:
        denom = denom + e
    gate_out = jnp.zeros((tm, LANES), F32)
    for k in range(TOP_K):
        gate_out = jnp.where(lane == k, exps[k] / denom, gate_out)
    base = carry_ref[...]
    rank_out = jnp.zeros((tm, LANES), I32)
    tril = tril_ref[...]
    for k in range(TOP_K):
        onehot = jnp.where(sels[k], 1.0, 0.0)
        before = jnp.dot(tril, onehot.astype(BF16), preferred_element_type=F32) + base
        rank = jnp.sum(onehot * before, axis=-1, keepdims=True)
        rank_out = jnp.where(lane == k, rank.astype(I32), rank_out)
        base = base + jnp.sum(onehot, axis=0, keepdims=True)
    carry_ref[...] = base
    idx_ref[...] = idx_out
    gate_ref[...] = gate_out
    rank_ref[...] = rank_out
    cnt_ref[...] = base.astype(I32)


def _router(x16, w3, b3, layer, n_experts):
    m, d = x16.shape
    tm = _pick(m, 256, 16)
    tril = jnp.tri(tm, k=-1, dtype=BF16)
    row = pl.BlockSpec((tm, LANES), lambda i: (i, 0))
    est = 2 * (tm * d * 2 + d * LANES * 2 + tm * tm * 2 + 3 * tm * LANES * 4) + 16 * tm * LANES * 4
    return pl.pallas_call(
        functools.partial(_router_kernel, n_experts=n_experts),
        out_shape=(jax.ShapeDtypeStruct((m, LANES), I32), jax.ShapeDtypeStruct((m, LANES), F32),
                   jax.ShapeDtypeStruct((m, LANES), I32), jax.ShapeDtypeStruct((1, LANES), I32)),
        grid=(m // tm,),
        in_specs=[pl.BlockSpec((tm, d), lambda i: (i, 0)),
                  pl.BlockSpec((None, d, LANES), lambda i: (layer, 0, 0)),
                  pl.BlockSpec((None, 1, LANES), lambda i: (layer, 0, 0)),
                  pl.BlockSpec((tm, tm), lambda i: (0, 0))],
        out_specs=(row, row, row, pl.BlockSpec((1, LANES), lambda i: (0, 0))),
        scratch_shapes=[pltpu.VMEM((1, LANES), F32)],
        compiler_params=_params(est, ("arbitrary",)),
        name="router",
    )(x16, w3, b3, tril)


def _gather_rows(src_hbm, idx_smem, slot, buf, sem, n, unroll=8):
    def body(j, _):
        t = idx_smem[slot, j]
        pltpu.make_async_copy(src_hbm.at[t], buf.at[slot, pl.ds(j, 1)], sem.at[slot]).start()
        return 0
    lax.fori_loop(0, n, body, 0, unroll=unroll)


def _wait_rows(src_hbm, slot, buf, sem, n, unroll=8):
    def body(j, _):
        pltpu.make_async_copy(src_hbm.at[0], buf.at[slot, pl.ds(j, 1)], sem.at[slot]).wait()
        return 0
    lax.fori_loop(0, n, body, 0, unroll=unroll)


def _dispatch_kernel(tok_ref, x_hbm, o_ref, idx_smem, buf, isem, gsem, *, tmb, n_blk):
    rb = pl.program_id(0)
    slot = rb % 2
    nxt = 1 - slot

    def idx_copy(blk, s):
        return pltpu.make_async_copy(tok_ref.at[blk], idx_smem.at[s], isem.at[s])

    @pl.when(rb == 0)
    def _():
        idx_copy(0, 0).start()
        idx_copy(0, 0).wait()
        _gather_rows(x_hbm, idx_smem, 0, buf, gsem, tmb)
        if n_blk > 1:
            idx_copy(1, 1).start()

    @pl.when(rb + 1 < n_blk)
    def _():
        idx_copy(rb + 1, nxt).wait()
        _gather_rows(x_hbm, idx_smem, nxt, buf, gsem, tmb)

    @pl.when(rb + 2 < n_blk)
    def _():
        idx_copy(rb + 2, slot).start()

    _wait_rows(x_hbm, slot, buf, gsem, tmb)
    o_ref[...] = buf[slot].astype(o_ref.dtype)


def _dispatch(row_tok2, x_rows, tmb):
    n_blk = row_tok2.shape[0]
    d = x_rows.shape[-1]
    est = 2 * tmb * d * 4 + 2 * tmb * d * 2 + row_tok2.size * 4 * 2 + tmb * d * 4
    return pl.pallas_call(
        functools.partial(_dispatch_kernel, tmb=tmb, n_blk=n_blk),
        out_shape=jax.ShapeDtypeStruct((n_blk * tmb, d), BF16),
        grid=(n_blk,),
        in_specs=[pl.BlockSpec((n_blk, tmb), lambda i: (0, 0)),
                  pl.BlockSpec(memory_space=pl.ANY)],
        out_specs=pl.BlockSpec((tmb, d), lambda i: (i, 0)),
        scratch_shapes=[pltpu.SMEM((2, tmb), I32), pltpu.VMEM((2, tmb, d), F32),
                        pltpu.SemaphoreType.DMA((2,)), pltpu.SemaphoreType.DMA((2,))],
        compiler_params=_params(est, ("arbitrary",)),
        name="moe_dispatch",
    )(row_tok2, x_rows)


def _expert_changed(meta_ref, rb):
    prev = meta_ref[1 + jnp.maximum(rb - 1, 0)]
    return jnp.logical_or(rb == 0, meta_ref[1 + rb] != prev)


def _expert_up_kernel(meta_ref, x_ref, wg_ref, wu_ref, bg_ref, bu_ref, h_ref, wgb_ref, wub_ref):
    rb = pl.program_id(1)
    live = rb < meta_ref[0]

    @pl.when(jnp.logical_and(live, _expert_changed(meta_ref, rb)))
    def _():
        wgb_ref[...] = wg_ref[...].astype(BF16)
        wub_ref[...] = wu_ref[...].astype(BF16)

    @pl.when(live)
    def _():
        tmb = x_ref.shape[0]
        rc = _pick(tmb, 256)
        for r in range(tmb // rc):
            rs = slice(r * rc, (r + 1) * rc)
            x = x_ref[rs, :]
            gate = jnp.dot(x, wgb_ref[...], preferred_element_type=F32) + bg_ref[...]
            up = jnp.dot(x, wub_ref[...], preferred_element_type=F32) + bu_ref[...]
            gate = jnp.minimum(gate, SWIGLU_LIMIT)
            up = jnp.clip(up, -SWIGLU_LIMIT, SWIGLU_LIMIT)
            h_ref[rs, :] = (gate * jax.nn.sigmoid(SWIGLU_ALPHA * gate) * (up + 1.0)).astype(h_ref.dtype)

    @pl.when(jnp.logical_not(live))
    def _():
        h_ref[...] = jnp.zeros(h_ref.shape, h_ref.dtype)


def _expert_up(meta, xs, w_gate_up, b_gate_up4, layer, tmb):
    n_rows, d = xs.shape
    n_blk = n_rows // tmb
    ff = w_gate_up.shape[-1] // 2
    tf = _pick(ff, 512, LANES)
    nf = ff // tf
    live_blk = lambda rb, m: jnp.minimum(rb, m[0] - 1)
    w_spec = lambda off: pl.BlockSpec((None, None, d, tf), lambda f, rb, m: (layer, m[1 + rb], 0, off + f))
    b_spec = lambda off: pl.BlockSpec((None, None, 1, tf), lambda f, rb, m: (layer, m[1 + rb], 0, off + f))
    est = 2 * (tmb * d * 2 + 2 * d * tf * 4 + tmb * tf * 2) + 2 * d * tf * 2 + 6 * tmb * tf * 4
    return pl.pallas_call(
        _expert_up_kernel,
        out_shape=jax.ShapeDtypeStruct((n_rows, ff), BF16),
        grid_spec=pltpu.PrefetchScalarGridSpec(
            num_scalar_prefetch=1,
            grid=(nf, n_blk),
            in_specs=[pl.BlockSpec((tmb, d), lambda f, rb, m: (live_blk(rb, m), 0)),
                      w_spec(0), w_spec(nf), b_spec(0), b_spec(nf)],
            out_specs=pl.BlockSpec((tmb, tf), lambda f, rb, m: (rb, f)),
            scratch_shapes=[pltpu.VMEM((d, tf), BF16), pltpu.VMEM((d, tf), BF16)]),
        compiler_params=_params(est, ("arbitrary", "arbitrary")),
        name="expert_up",
    )(meta, xs, w_gate_up, w_gate_up, b_gate_up4, b_gate_up4)


def _expert_down_kernel(meta_ref, h_ref, w_ref, b_ref, y_ref, wb_ref):
    rb = pl.program_id(1)
    live = rb < meta_ref[0]

    @pl.when(jnp.logical_and(live, _expert_changed(meta_ref, rb)))
    def _():
        wb_ref[...] = w_ref[...].astype(BF16)

    @pl.when(live)
    def _():
        tmb, _, tn = y_ref.shape
        rc = _pick(tmb, 128)
        for r in range(tmb // rc):
            rs = slice(r * rc, (r + 1) * rc)
            y = jnp.dot(h_ref[rs, :], wb_ref[...], preferred_element_type=F32) + b_ref[...]
            y_ref[rs] = y.reshape(rc, 1, tn)

    @pl.when(jnp.logical_not(live))
    def _():
        y_ref[...] = jnp.zeros(y_ref.shape, y_ref.dtype)


def _expert_down(meta, hs, w_down, b_down4, layer, tmb):
    n_rows, ff = hs.shape
    n_blk = n_rows // tmb
    d = w_down.shape[-1]
    tn = _pick(d, 2048, LANES)
    live_blk = lambda rb, m: jnp.minimum(rb, m[0] - 1)
    est = 2 * (tmb * ff * 2 + ff * tn * 4 + tmb * tn * 4) + ff * tn * 2 + 2 * tmb * tn * 4
    return pl.pallas_call(
        _expert_down_kernel,
        out_shape=jax.ShapeDtypeStruct((n_rows, 1, d), F32),
        grid_spec=pltpu.PrefetchScalarGridSpec(
            num_scalar_prefetch=1,
            grid=(d // tn, n_blk),
            in_specs=[pl.BlockSpec((tmb, ff), lambda c, rb, m: (live_blk(rb, m), 0)),
                      pl.BlockSpec((None, None, ff, tn), lambda c, rb, m: (layer, m[1 + rb], 0, c)),
                      pl.BlockSpec((None, None, 1, tn), lambda c, rb, m: (layer, m[1 + rb], 0, c))],
            out_specs=pl.BlockSpec((tmb, 1, tn), lambda c, rb, m: (rb, 0, c)),
            scratch_shapes=[pltpu.VMEM((ff, tn), BF16)]),
        compiler_params=_params(est, ("arbitrary", "arbitrary")),
        name="expert_down",
    )(meta, hs, w_down, b_down4)


def _combine_kernel(dest_ref, y_hbm, gate_ref, h_ref, g_ref, b_ref, o32_ref, o16_ref,
                    idx_smem, buf, isem, gsem, *, tmc, n_blk, alpha):
    i = pl.program_id(0)
    slot = i % 2
    nxt = 1 - slot
    n = TOP_K * tmc

    def idx_copy(blk, s):
        return pltpu.make_async_copy(dest_ref.at[blk], idx_smem.at[s], isem.at[s])

    @pl.when(i == 0)
    def _():
        idx_copy(0, 0).start()
        idx_copy(0, 0).wait()
        _gather_rows(y_hbm, idx_smem, 0, buf, gsem, n)
        if n_blk > 1:
            idx_copy(1, 1).start()

    @pl.when(i + 1 < n_blk)
    def _():
        idx_copy(i + 1, nxt).wait()
        _gather_rows(y_hbm, idx_smem, nxt, buf, gsem, n)

    @pl.when(i + 2 < n_blk)
    def _():
        idx_copy(i + 2, slot).start()

    _wait_rows(y_hbm, slot, buf, gsem, n)
    gates = gate_ref[...]
    mix = None
    for k in range(TOP_K):
        part = gates[:, k:k + 1] * buf[slot, k * tmc:(k + 1) * tmc, :]
        mix = part if mix is None else mix + part
    out = _layer_norm(alpha * h_ref[...] + mix, g_ref[...], b_ref[...])
    o32_ref[...] = out
    o16_ref[...] = out.astype(BF16)


def _combine_ln(dest2, y, gates, h32, g3, b3, layer, alpha, tmc):
    m, d = h32.shape
    n_blk = m // tmc
    n = TOP_K * tmc
    row = pl.BlockSpec((tmc, d), lambda i: (i, 0))
    par = pl.BlockSpec((None, 1, d), lambda i: (layer, 0, 0))
    est = 2 * n * d * 4 + 2 * (tmc * d * 10 + tmc * LANES * 4) + dest2.size * 4 * 2 + 4 * tmc * d * 4
    return pl.pallas_call(
        functools.partial(_combine_kernel, tmc=tmc, n_blk=n_blk, alpha=alpha),
        out_shape=(jax.ShapeDtypeStruct((m, d), F32), jax.ShapeDtypeStruct((m, d), BF16)),
        grid=(n_blk,),
        in_specs=[pl.BlockSpec((n_blk, n), lambda i: (0, 0)),
                  pl.BlockSpec(memory_space=pl.ANY),
                  pl.BlockSpec((tmc, LANES), lambda i: (i, 0)),
                  row, par, par],
        out_specs=(row, row),
        scratch_shapes=[pltpu.SMEM((2, n), I32), pltpu.VMEM((2, n, d), F32),
                        pltpu.SemaphoreType.DMA((2,)), pltpu.SemaphoreType.DMA((2,))],
        compiler_params=_params(est, ("arbitrary",)),
        name="moe_combine_ln",
    )(dest2, y, gates, h32, g3, b3)


def _route_tables(idx, rank, counts, tmb, n_blk, tmc):
    t = idx.shape[0]
    n_exp = counts.shape[0]
    padded = (counts + tmb - 1) // tmb * tmb
    pad_end = jnp.cumsum(padded)
    pad_start = pad_end - padded
    onehot = idx[..., None] == jnp.arange(n_exp, dtype=I32)
    dest = rank + jnp.sum(jnp.where(onehot, pad_start, 0), axis=-1)
    n_used = pad_end[-1] // tmb
    blk = jnp.arange(n_blk, dtype=I32)
    blk_e = jnp.minimum(jnp.sum(pad_end[None, :] <= (blk * tmb)[:, None], axis=1), n_exp - 1).astype(I32)
    blk_e = jnp.where(blk < n_used, blk_e, blk_e[jnp.maximum(n_used - 1, 0)])
    meta = jnp.concatenate([n_used[None].astype(I32), blk_e])
    tok = jnp.broadcast_to(jnp.arange(t, dtype=I32)[:, None], dest.shape)
    row_tok = jnp.zeros((n_blk * tmb,), I32).at[dest.reshape(-1)].set(tok.reshape(-1), unique_indices=True)
    dest2 = dest.reshape(t // tmc, tmc, TOP_K).transpose(0, 2, 1).reshape(t // tmc, TOP_K * tmc)
    return meta, row_tok.reshape(n_blk, tmb), dest2


def kernel(x_prompt, x_sample, mem_prompt, cache_fox_k, cache_fox_v, cache_fox_logf, cache_pool, cache_conv,
           cache_mem_k, cache_mem_v, emb_ln_g, emb_ln_b, w_in, b_fgate, w_pool, pool_scale, w_dw, b_dw,
           conv_ln_g, conv_ln_b, w_pw, b_pw, gmlp_ln_g, gmlp_ln_b, w_spatial, b_spatial, group_norm_g, w_out,
           ln1_g, ln1_b, w_xq, w_xk, w_xv, w_xo, ln2_g, ln2_b, w_router, b_router, w_gate_up, b_gate_up,
           w_down, b_down, ln3_g, ln3_b):
    bp, seq, d = x_prompt.shape
    bs, ls, _ = x_sample.shape
    depth = w_in.shape[0]
    gw = d // 4
    heads = gw // HEAD_DIM
    past = cache_fox_k.shape[2]
    n_mem = mem_prompt.shape[1]
    mw = w_xq.shape[-1]
    n_exp = w_router.shape[-1]
    tp, ts = bp * seq, bs * ls
    t = tp + ts
    alpha = float((2 * depth) ** 0.25)
    lc_p = min(w_spatial.shape[-1], seq)
    lc_s = min(w_spatial.shape[-1], ls)
    tq = _pick(seq, 256)
    nq = seq // tq

    vec3 = lambda a: a.reshape(depth, 1, a.shape[-1])
    rest0 = 3 * gw + heads
    w_rest16 = w_in[:, :, rest0:].astype(BF16)
    w_f = jnp.pad(w_in[:, :, 3 * gw:rest0], ((0, 0), (0, 0), (0, LANES - heads)))
    b_f = jnp.pad(b_fgate, ((0, 0), (0, LANES - heads))).reshape(depth, 1, LANES)
    n_win = len(POOL_WINDOWS)
    pg = gw // n_win
    wbd = jnp.zeros((depth, gw, gw), F32)
    for g in range(n_win):
        wbd = wbd.at[:, g * pg:(g + 1) * pg, g * pg:(g + 1) * pg].set(w_pool[:, g])
    wbd16 = wbd.astype(BF16)
    wpw16 = w_pw.astype(BF16)

    def spatial(lc):
        causal = jnp.tril(jnp.ones((lc, lc), dtype=bool))
        ws = jnp.where(causal[None, None], w_spatial[:, :, :lc, :lc], 0).astype(BF16)
        bias = jnp.repeat(jnp.transpose(b_spatial[:, :, :lc], (0, 2, 1)), HEAD_DIM, axis=-1)
        return ws, bias

    ws_p, bias_p = spatial(lc_p)
    ws_s, bias_s = spatial(lc_s)
    w_router16 = jnp.pad(w_router, ((0, 0), (0, 0), (0, LANES - n_exp))).astype(BF16)
    b_router3 = jnp.pad(b_router, ((0, 0), (0, LANES - n_exp))).reshape(depth, 1, LANES)
    b_gu4 = b_gate_up.reshape(depth, n_exp, 1, b_gate_up.shape[-1])
    b_dn4 = b_down.reshape(depth, n_exp, 1, d)
    gn3 = vec3(group_norm_g)
    cache_k4 = cache_fox_k.reshape(depth, bs, past, gw)
    cache_v4 = cache_fox_v.reshape(depth, bs, past, gw)
    cache_mk = cache_mem_k.reshape(depth * bs * n_mem, mw)
    cache_mv = cache_mem_v.reshape(depth * bs * n_mem, mw)
    mem16 = mem_prompt.reshape(bp * n_mem, d).astype(BF16)

    tmb = 512 if t >= 4096 else 64
    n_blk = -(-(t * TOP_K + n_exp * (tmb - 1)) // tmb)
    tmc = _pick(t, 64)

    x_all = jnp.concatenate([x_prompt.reshape(tp, d), x_sample.reshape(ts, d)], axis=0)
    x32, x16 = _add_ln(x_all, None, emb_ln_g.reshape(1, 1, d), emb_ln_b.reshape(1, 1, d), 0, alpha)

    outs = [[] for _ in range(13)]
    for l in range(depth):
        (q16,) = _dense_mm([x16], w_in, l, 0, gw, (BF16,))
        kv32, kv16 = _dense_mm([x16], w_in, l, gw, 2 * gw, (F32, BF16))
        (zr,) = _dense_mm([x16], w_rest16, l, 0, 5 * gw, (BF16,))
        (logf_pad,) = _dense_mm([x16], w_f, l, 0, LANES, (F32,), bias3=b_f, epilogue="log_sigmoid")
        logf = logf_pad[:, :heads]

        lf_p = logf[:tp].reshape(bp, seq, heads)
        lf_s = logf[tp:].reshape(bs, ls, heads)
        cum_parts = _cumsum_time(jnp.transpose(lf_p, (1, 0, 2)).reshape(seq, bp * heads),
                                 split_scale=HEAD_DIM ** 0.5)
        u3 = jnp.stack([jnp.transpose(c.reshape(seq, bp, heads), (1, 0, 2)) for c in cum_parts[1:]], axis=-1)
        ones3 = jnp.ones_like(u3)
        fill = jnp.zeros(u3.shape[:-1] + (HEAD_DIM - 6,), BF16)
        aug_q = jnp.concatenate([u3, ones3, fill], axis=-1).reshape(tp, gw)
        aug_k = jnp.concatenate([ones3, -u3, fill], axis=-1).reshape(tp, gw)
        lk = past + ls
        lkp = -(-lk // 256) * 256
        lf_all = jnp.concatenate([cache_fox_logf[l].astype(F32), lf_s], axis=1)
        lf_all = jnp.pad(jnp.transpose(lf_all, (1, 0, 2)).reshape(lk, bs * heads), ((0, lkp - lk), (0, 0)))
        cum_s = jnp.transpose(_cumsum_time(lf_all)[0][:lk].reshape(lk, bs, heads), (1, 0, 2))
        cq_s = cum_s[:, past:].reshape(ts, heads)
        ck_s = jnp.transpose(cum_s, (0, 2, 1))
        tkc = _pick(past, 512)
        ckc = ck_s[:, :, :past].reshape(bs, heads, past // tkc, tkc)
        ckn = ck_s[:, :, past:]

        ya_p = _fox_prompt(q16, kv16, aug_q, aug_k, gn3, l, bp, seq, tq)
        ya_s = _fox_sample(q16, kv16, cache_k4, cache_v4, cq_s, ckc, ckn, gn3, l, tp, bs, ls)
        yb_p, pool_p = _pool_mixer(zr, 0, None, wbd16, vec3(pool_scale), gn3, 1, l, 0, bp, seq, 0)
        yb_s, pool_s = _pool_mixer(zr, 0, cache_pool, wbd16, vec3(pool_scale), gn3, 1, l, tp, bs, ls, past)
        conv_args = (w_dw, vec3(b_dw), vec3(conv_ln_g), vec3(conv_ln_b), wpw16, vec3(b_pw), gn3, 2, l)
        yc_p, conv_p = _conv_mixer(zr, 1, None, *conv_args, 0, bp, seq)
        yc_s, conv_s = _conv_mixer(zr, 1, cache_conv, *conv_args, tp, bs, ls)
        (yd_p,) = _gmlp_mixer(zr, 3, vec3(gmlp_ln_g), vec3(gmlp_ln_b), ws_p, bias_p, gn3, 3, l, 0, tp, lc_p, False)
        yd_s, gv_s = _gmlp_mixer(zr, 3, vec3(gmlp_ln_g), vec3(gmlp_ln_b), ws_s, bias_s, gn3, 3, l, tp, ts, lc_s, True)
        (mix,) = _dense_mm([(ya_p, ya_s), (yb_p, yb_s), (yc_p, yc_s), (yd_p, yd_s)], w_out, l, 0, d, (F32,))
        h1_32, h1_16 = _add_ln(x32, mix, vec3(ln1_g), vec3(ln1_b), l, alpha)
        (qm16,) = _dense_mm([h1_16], w_xq, l, 0, mw, (BF16,))
        (mk,) = _dense_mm([mem16], w_xk, l, 0, mw, (F32,))
        (mv,) = _dense_mm([mem16], w_xv, l, 0, mw, (F32,))
        om_p = _mem_attention(qm16, mk, mv, n_mem, 0, tp, seq, 0)
        om_s = _mem_attention(qm16, cache_mk, cache_mv, n_mem, tp, ts, ls, l * bs)
        (xo,) = _dense_mm([(om_p, om_s)], w_xo, l, 0, d, (F32,))
        h2_32, h2_16, h2_rows = _add_ln(h1_32, xo, vec3(ln2_g), vec3(ln2_b), l, alpha, emit_rows=True)

        idx_pad, gates, rank_pad, cnt_pad = _router(h2_16, w_router16, b_router3, l, n_exp)
        meta, row_tok2, dest2 = _route_tables(idx_pad[:, :TOP_K], rank_pad[:, :TOP_K], cnt_pad[0, :n_exp],
                                              tmb, n_blk, tmc)
        xs = _dispatch(row_tok2, h2_rows, tmb)
        hs = _expert_up(meta, xs, w_gate_up, b_gu4, l, tmb)
        y = _expert_down(meta, hs, w_down, b_dn4, l, tmb)
        x32, x16 = _combine_ln(dest2, y, gates, h2_32, vec3(ln3_g), vec3(ln3_b), l, alpha, tmc)

        k32, v32 = kv32[:, :gw], kv32[:, gw:]
        per_layer = (
            k32[:tp].reshape(bp, seq, heads, HEAD_DIM), v32[:tp].reshape(bp, seq, heads, HEAD_DIM), lf_p,
            pool_p, conv_p, mk.reshape(bp, n_mem, mw // HEAD_DIM, HEAD_DIM),
            mv.reshape(bp, n_mem, mw // HEAD_DIM, HEAD_DIM),
            k32[tp:].reshape(bs, ls, heads, HEAD_DIM), v32[tp:].reshape(bs, ls, heads, HEAD_DIM), lf_s,
            pool_s, conv_s, gv_s.reshape(bs, ls, heads, HEAD_DIM))
        for acc, val in zip(outs, per_layer):
            acc.append(val)

    return (x32[:tp].reshape(bp, seq, d), x32[tp:].reshape(bs, ls, d)) + tuple(jnp.stack(o) for o in outs)
```

```python
import functools
import math

import jax
import jax.numpy as jnp
from jax import lax
from jax.experimental import pallas as pl
from jax.experimental.pallas import tpu as pltpu

F32 = jnp.float32
BF16 = jnp.bfloat16
I32 = jnp.int32

HEAD_DIM = 128
LANES = 128
LN_EPS = 1e-5
NEG_INF = -1e30
POOL_WINDOWS = (2, 4, 8, 16)
TOP_K = 4
SWIGLU_LIMIT = 7.0
SWIGLU_ALPHA = 1.702
VMEM_CAP = 60 * 1024 * 1024


def _pick(n, target, mult=8):
    if n <= target:
        return n
    t = target - target % mult
    while t >= mult:
        if n % t == 0:
            return t
        t -= mult
    return n


def _params(est_bytes, semantics):
    limit = int(min(VMEM_CAP, max(16 * 1024 * 1024, est_bytes * 5 // 4 + (2 << 20))))
    return pltpu.CompilerParams(dimension_semantics=semantics, vmem_limit_bytes=limit)


def _layer_norm(x, g, b):
    mu = jnp.mean(x, axis=-1, keepdims=True)
    xc = x - mu
    var = jnp.mean(xc * xc, axis=-1, keepdims=True)
    return xc * lax.rsqrt(var + LN_EPS) * g + b


def _rms_gain(y, g):
    return y * lax.rsqrt(jnp.mean(y * y, axis=-1, keepdims=True) + LN_EPS) * g


def _log_sigmoid(x):
    return -(jnp.maximum(-x, 0.0) + jnp.log1p(jnp.exp(-jnp.abs(x))))


def _add_ln_kernel(*refs, alpha, has_res, emit_rows):
    n_in = 4 if has_res else 3
    ins, outs = refs[:n_in], refs[n_in:]
    if has_res:
        x_ref, y_ref, g_ref, b_ref = ins
        x = alpha * x_ref[...] + y_ref[...].astype(F32)
    else:
        x_ref, g_ref, b_ref = ins
        x = x_ref[...]
    out = _layer_norm(x, g_ref[...], b_ref[...])
    outs[0][...] = out
    outs[1][...] = out.astype(BF16)
    if emit_rows:
        outs[2][...] = out.reshape(outs[2].shape)


def _add_ln(x, y, g3, b3, layer, alpha, emit_rows=False):
    m, d = x.shape
    tm = _pick(m, 256)
    row = pl.BlockSpec((tm, d), lambda i: (i, 0))
    par = pl.BlockSpec((None, 1, d), lambda i: (layer, 0, 0))
    ins = [x] + ([y] if y is not None else []) + [g3, b3]
    specs = [row] + ([row] if y is not None else []) + [par, par]
    out_shape = [jax.ShapeDtypeStruct((m, d), F32), jax.ShapeDtypeStruct((m, d), BF16)]
    out_specs = [row, row]
    if emit_rows:
        out_shape.append(jax.ShapeDtypeStruct((m, 1, d), F32))
        out_specs.append(pl.BlockSpec((tm, 1, d), lambda i: (i, 0, 0)))
    est = tm * d * (4 * (2 if y is not None else 1) + 4 + 2 + (4 if emit_rows else 0)) * 2 + tm * d * 8
    return pl.pallas_call(
        functools.partial(_add_ln_kernel, alpha=alpha, has_res=y is not None, emit_rows=emit_rows),
        out_shape=tuple(out_shape),
        grid=(m // tm,),
        in_specs=specs,
        out_specs=tuple(out_specs),
        compiler_params=_params(est, ("arbitrary",)),
        name="add_ln",
    )(*ins)


def _mm_kernel(*refs, ks, cast_w, has_bias, epilogue, n_out, n_head_blk):
    n_a = len(ks)
    n_groups = 2 if n_head_blk is not None else 1
    a_groups = [refs[g * n_a:(g + 1) * n_a] for g in range(n_groups)]
    w_ref = refs[n_groups * n_a]
    pos = n_groups * n_a + 1
    bias_ref = None
    if has_bias:
        bias_ref = refs[pos]
        pos += 1
    o_refs = refs[pos:pos + n_out]
    if cast_w:
        wb_ref = refs[pos + n_out]

        @pl.when(pl.program_id(1) == 0)
        def _():
            wb_ref[...] = w_ref[...].astype(BF16)
        w = wb_ref
    else:
        w = w_ref

    def run(a_refs):
        acc = None
        off = 0
        for a_ref, k in zip(a_refs, ks):
            part = jnp.dot(a_ref[...], w[off:off + k, :], preferred_element_type=F32)
            acc = part if acc is None else acc + part
            off += k
        if has_bias:
            acc = acc + bias_ref[...]
        if epilogue == "log_sigmoid":
            acc = _log_sigmoid(acc)
        for o_ref in o_refs:
            o_ref[...] = acc.astype(o_ref.dtype)

    if n_head_blk is None:
        run(a_groups[0])
    else:
        i = pl.program_id(1)
        pl.when(i < n_head_blk)(lambda: run(a_groups[0]))
        pl.when(i >= n_head_blk)(lambda: run(a_groups[1]))


def _dense_mm(a_list, w3, layer, col_off, n, out_dtypes, *, bias3=None, epilogue=None,
              tm_target=512, tn_target=512):
    split = isinstance(a_list[0], tuple)
    heads_ = [a[0] if split else a for a in a_list]
    ks = tuple(a.shape[1] for a in heads_)
    k = sum(ks)
    cast_w = w3.dtype != BF16
    if split:
        m_head, m_tail = a_list[0][0].shape[0], a_list[0][1].shape[0]
        m = m_head + m_tail
        tm = _pick(math.gcd(m_head, m_tail), tm_target, 16)
        n_head_blk = m_head // tm
        last_tail = m_tail // tm - 1
        a_specs = ([pl.BlockSpec((tm, kg), lambda j, i: (jnp.minimum(i, n_head_blk - 1), 0)) for kg in ks]
                   + [pl.BlockSpec((tm, kg), lambda j, i: (jnp.clip(i - n_head_blk, 0, last_tail), 0)) for kg in ks])
        ins = [a[0] for a in a_list] + [a[1] for a in a_list] + [w3]
    else:
        m = heads_[0].shape[0]
        tm = _pick(m, tm_target, 16)
        n_head_blk = None
        a_specs = [pl.BlockSpec((tm, kg), lambda j, i: (i, 0)) for kg in ks]
        ins = list(a_list) + [w3]
    tn = _pick(math.gcd(n, col_off), tn_target, LANES)
    assert col_off % tn == 0 and n % tn == 0
    off_blk = col_off // tn
    w_spec = pl.BlockSpec((None, k, tn), lambda j, i: (layer, 0, off_blk + j))
    specs = a_specs + [w_spec]
    if bias3 is not None:
        ins.append(bias3)
        specs.append(pl.BlockSpec((None, 1, tn), lambda j, i: (layer, 0, off_blk + j)))
    out_spec = pl.BlockSpec((tm, tn), lambda j, i: (i, j))
    out_shape = tuple(jax.ShapeDtypeStruct((m, n), dt) for dt in out_dtypes)
    scratch = [pltpu.VMEM((k, tn), BF16)] if cast_w else []
    est = ((4 if split else 2) * tm * k * 2 + 2 * k * tn * w3.dtype.itemsize + (k * tn * 2 if cast_w else 0)
           + sum(2 * tm * tn * jnp.dtype(dt).itemsize for dt in out_dtypes) + 2 * tm * tn * 4)
    outs = pl.pallas_call(
        functools.partial(_mm_kernel, ks=ks, cast_w=cast_w, has_bias=bias3 is not None,
                          epilogue=epilogue, n_out=len(out_dtypes), n_head_blk=n_head_blk),
        out_shape=out_shape,
        grid=(n // tn, m // tm),
        in_specs=specs,
        out_specs=tuple(out_spec for _ in out_dtypes),
        scratch_shapes=scratch,
        compiler_params=_params(est, ("arbitrary", "arbitrary")),
        name="dense_mm",
    )(*ins)
    return outs


def _split3(x):
    hi = x.astype(BF16)
    r1 = x - hi.astype(F32)
    mid = r1.astype(BF16)
    lo = (r1 - mid.astype(F32)).astype(BF16)
    return hi, mid, lo


def _cumsum_kernel(x_ref, tri_ref, o_ref, *split_refs, blk, n_blk, split_scale):
    tri = tri_ref[...]
    carry = jnp.zeros((1, x_ref.shape[1]), F32)
    for r in range(n_blk):
        rs = slice(r * blk, (r + 1) * blk)
        hi, mid, lo = _split3(x_ref[rs, :])
        c = (jnp.dot(tri, hi, preferred_element_type=F32)
             + jnp.dot(tri, mid, preferred_element_type=F32)
             + jnp.dot(tri, lo, preferred_element_type=F32)) + carry
        o_ref[rs, :] = c
        if split_refs:
            for ref, term in zip(split_refs, _split3(c * split_scale)):
                ref[rs, :] = term
        carry = c[blk - 1:blk, :]


def _cumsum_time(x, split_scale=None):
    lp, c = x.shape
    blk = _pick(lp, 256)
    tri = jnp.tri(blk, dtype=BF16)
    full = lambda shp: pl.BlockSpec(shp, lambda i: (0,) * len(shp))
    n_split = 3 if split_scale is not None else 0
    out_shape = (jax.ShapeDtypeStruct((lp, c), F32),) + (jax.ShapeDtypeStruct((lp, c), BF16),) * n_split
    return pl.pallas_call(
        functools.partial(_cumsum_kernel, blk=blk, n_blk=lp // blk, split_scale=split_scale),
        out_shape=out_shape,
        grid=(1,),
        in_specs=[full((lp, c)), full((blk, blk))],
        out_specs=tuple(full((lp, c)) for _ in out_shape),
        compiler_params=_params(6 * lp * LANES * 4 * 2, ("arbitrary",)),
        name="cumsum_time",
    )(x, tri)


def _softmax_step(q, k, v, cq, ck, carry, scale, mask):
    m, l, acc = carry
    s = lax.dot_general(q, k, (((1,), (1,)), ((), ())), preferred_element_type=F32) * scale
    s = s + cq - ck
    if mask is not None:
        s = jnp.where(mask, s, NEG_INF)
    m_new = jnp.maximum(m, jnp.max(s, axis=-1, keepdims=True))
    p = jnp.exp(s - m_new)
    a = jnp.exp(m - m_new)
    l_new = a * l + jnp.sum(p, axis=-1, keepdims=True)
    acc_new = a * acc + jnp.dot(p.astype(BF16), v, preferred_element_type=F32)
    return m_new, l_new, acc_new


def _fox_prompt_kernel(q_ref, aq_ref, k_ref, ak_ref, v_ref, gn_ref, y_ref, vt_ref, m_ref, l_ref, acct_ref,
                       acc_ref, s_ref, p_ref, *, tq, heads, c2):
    qi = pl.program_id(1)
    nk = k_ref.shape[0] // tq

    @pl.when(qi == 0)
    def _():
        for h in range(heads):
            sl = slice(h * HEAD_DIM, (h + 1) * HEAD_DIM)
            for j in range(nk):
                vt_ref[h, j] = v_ref[j * tq:(j + 1) * tq, sl].T

    m_ref[...] = jnp.full(m_ref.shape, NEG_INF, F32)
    l_ref[...] = jnp.zeros(l_ref.shape, F32)
    acct_ref[...] = jnp.zeros(acct_ref.shape, F32)

    def key_block(j, mask):
        ks = pl.multiple_of(j * tq, tq)
        for h in range(heads):
            sl = slice(h * HEAD_DIM, (h + 1) * HEAD_DIM)
            k2 = jnp.concatenate([k_ref[pl.ds(ks, tq), sl], ak_ref[pl.ds(ks, tq), sl]], axis=1)
            q2 = jnp.concatenate([q_ref[:, sl], aq_ref[:, sl]], axis=1)
            s_ref[h] = lax.dot_general(k2, q2, (((1,), (1,)), ((), ())), preferred_element_type=F32)
        for h in range(heads):
            s = s_ref[h] * c2
            if mask is not None:
                s = jnp.where(mask, s, NEG_INF)
            m_old = m_ref[h]
            m_new = jnp.maximum(m_old, jnp.max(s, axis=0, keepdims=True))
            p = jnp.exp2(s - m_new)
            a = jnp.exp2(m_old - m_new)
            l_ref[h] = a * l_ref[h] + jnp.sum(p, axis=0, keepdims=True)
            p_ref[h] = p.astype(BF16)
            acct_ref[h] = a * acct_ref[h]
            m_ref[h] = m_new
        for h in range(heads):
            acct_ref[h] += jnp.dot(vt_ref[h, j], p_ref[h], preferred_element_type=F32)

    def body(j, c):
        key_block(j, None)
        return c

    lax.fori_loop(0, qi, body, 0)
    key_idx = lax.broadcasted_iota(I32, (tq, tq), 0)
    qry_idx = lax.broadcasted_iota(I32, (tq, tq), 1)
    key_block(qi, key_idx <= qry_idx)
    for h in range(heads):
        sl = slice(h * HEAD_DIM, (h + 1) * HEAD_DIM)
        acc_ref[:, sl] = (acct_ref[h] / l_ref[h]).T
    y_ref[...] = _rms_gain(acc_ref[...], gn_ref[...]).astype(y_ref.dtype)


def _fox_prompt(q16, kv16, aug_q, aug_k, gn3, layer, n_seq, seq, tq):
    gw = q16.shape[1]
    heads = gw // HEAD_DIM
    nq = seq // tq
    qblk = pl.BlockSpec((tq, gw), lambda b, i: (b * nq + i, 0))
    est = 2 * (3 * tq * gw * 2 + 3 * seq * gw * 2) + seq * gw * 2 + tq * gw * 8 + 3 * heads * 8 * tq * 4 \
        + 8 * tq * tq * 4
    return pl.pallas_call(
        functools.partial(_fox_prompt_kernel, tq=tq, heads=heads, c2=HEAD_DIM ** -0.5 * math.log2(math.e)),
        out_shape=jax.ShapeDtypeStruct((n_seq * seq, gw), BF16),
        grid=(n_seq, nq),
        in_specs=[
            qblk, qblk,
            pl.BlockSpec((seq, gw), lambda b, i: (b, 0)),
            pl.BlockSpec((seq, gw), lambda b, i: (b, 0)),
            pl.BlockSpec((seq, gw), lambda b, i: (b, 1)),
            pl.BlockSpec((None, 1, gw), lambda b, i: (layer, 0, 0)),
        ],
        out_specs=qblk,
        scratch_shapes=[pltpu.VMEM((heads, nq, HEAD_DIM, tq), BF16),
                        pltpu.VMEM((heads, 1, tq), F32), pltpu.VMEM((heads, 1, tq), F32),
                        pltpu.VMEM((heads, HEAD_DIM, tq), F32), pltpu.VMEM((tq, gw), F32),
                        pltpu.VMEM((heads, tq, tq), F32), pltpu.VMEM((heads, tq, tq), BF16)],
        compiler_params=_params(est + heads * tq * tq * 6, ("arbitrary", "arbitrary")),
        name="fox_prompt",
    )(q16, aug_q, kv16, aug_k, kv16, gn3)


def _fox_sample_kernel(q_ref, kc_ref, vc_ref, kn_ref, vn_ref, cq_ref, ckc_ref, ckn_ref, gn_ref, y_ref,
                       m_ref, l_ref, acc_ref, *, ls, heads, scale, n_kc):
    kc = pl.program_id(1)

    @pl.when(kc == 0)
    def _():
        m_ref[...] = jnp.full(m_ref.shape, NEG_INF, F32)
        l_ref[...] = jnp.zeros(l_ref.shape, F32)
        acc_ref[...] = jnp.zeros(acc_ref.shape, F32)

    def update(h, k, v, ck, mask):
        sl = slice(h * HEAD_DIM, (h + 1) * HEAD_DIM)
        carry = (m_ref[:, h:h + 1], l_ref[:, h:h + 1], acc_ref[:, sl])
        m, l, acc = _softmax_step(q_ref[:, sl], k, v, cq_ref[:, h:h + 1], ck, carry, scale, mask)
        m_ref[:, h:h + 1] = m
        l_ref[:, h:h + 1] = l
        acc_ref[:, sl] = acc

    for h in range(heads):
        sl = slice(h * HEAD_DIM, (h + 1) * HEAD_DIM)
        update(h, kc_ref[:, sl].astype(BF16), vc_ref[:, sl].astype(BF16), ckc_ref[h, pl.ds(kc, 1), :], None)

    @pl.when(kc == n_kc - 1)
    def _():
        row = lax.broadcasted_iota(I32, (ls, ls), 0)
        col = lax.broadcasted_iota(I32, (ls, ls), 1)
        for h in range(heads):
            sl = slice(h * HEAD_DIM, (h + 1) * HEAD_DIM)
            update(h, kn_ref[:, sl], vn_ref[:, sl], ckn_ref[h:h + 1, :], col <= row)
        for h in range(heads):
            sl = slice(h * HEAD_DIM, (h + 1) * HEAD_DIM)
            acc_ref[:, sl] = acc_ref[:, sl] / l_ref[:, h:h + 1]
        y_ref[...] = _rms_gain(acc_ref[...], gn_ref[...]).astype(y_ref.dtype)


def _fox_sample(q16, kv16, cache_k, cache_v, cq, ckc4, ckn, gn3, layer, row0, n_seq, ls):
    gw = q16.shape[1]
    heads = gw // HEAD_DIM
    past = cache_k.shape[2]
    tkc = _pick(past, 512)
    n_kc = past // tkc
    blk0 = row0 // ls
    est = 2 * (3 * ls * gw * 2 + 2 * tkc * gw * 4 + heads * past * 4 + ls * gw * 2) + 3 * ls * gw * 4 \
        + 2 * tkc * gw * 2 + 6 * ls * tkc * 4
    return pl.pallas_call(
        functools.partial(_fox_sample_kernel, ls=ls, heads=heads, scale=HEAD_DIM ** -0.5, n_kc=n_kc),
        out_shape=jax.ShapeDtypeStruct((n_seq * ls, gw), BF16),
        grid=(n_seq, n_kc),
        in_specs=[
            pl.BlockSpec((ls, gw), lambda b, c: (blk0 + b, 0)),
            pl.BlockSpec((None, None, tkc, gw), lambda b, c: (layer, b, c, 0)),
            pl.BlockSpec((None, None, tkc, gw), lambda b, c: (layer, b, c, 0)),
            pl.BlockSpec((ls, gw), lambda b, c: (blk0 + b, 0)),
            pl.BlockSpec((ls, gw), lambda b, c: (blk0 + b, 1)),
            pl.BlockSpec((ls, heads), lambda b, c: (b, 0)),
            pl.BlockSpec((None, heads, n_kc, tkc), lambda b, c: (b, 0, 0, 0)),
            pl.BlockSpec((None, heads, ls), lambda b, c: (b, 0, 0)),
            pl.BlockSpec((None, 1, gw), lambda b, c: (layer, 0, 0)),
        ],
        out_specs=pl.BlockSpec((ls, gw), lambda b, c: (b, 0)),
        scratch_shapes=[pltpu.VMEM((ls, heads), F32), pltpu.VMEM((ls, heads), F32), pltpu.VMEM((ls, gw), F32)],
        compiler_params=_params(est, ("arbitrary", "arbitrary")),
        name="fox_sample",
    )(q16, cache_k, cache_v, kv16, kv16, cq, ckc4, ckn, gn3)


def _pool_kernel(*refs, tl, n_chunks, has_hist, pos0, hist_len):
    if has_hist:
        x_ref, hist_ref, wbd_ref, scale_ref, gn_ref, y_ref, state_ref, ext_ref = refs
    else:
        x_ref, wbd_ref, scale_ref, gn_ref, y_ref, state_ref, ext_ref = refs
    gw = x_ref.shape[1]
    pre = 16
    c = pl.program_id(1)

    @pl.when(c == 0)
    def _():
        ext_ref[0:pre, :] = jnp.zeros((pre, gw), F32)
        if has_hist:
            ext_ref[pre - hist_len:pre, :] = hist_ref[...]

    @pl.when(c > 0)
    def _():
        ext_ref[0:pre, :] = ext_ref[tl:tl + pre, :]

    x = x_ref[...].astype(F32)
    ext_ref[pre:pre + tl, :] = x
    pos = pos0 + c * tl + lax.broadcasted_iota(I32, (tl, 1), 0)
    pg = gw // len(POOL_WINDOWS)
    parts = []
    for g, w in enumerate(POOL_WINDOWS):
        cs = slice(g * pg, (g + 1) * pg)
        acc = x[:, cs]
        for j in range(1, w):
            acc = acc + ext_ref[pre - j:pre - j + tl, cs]
        cnt = jnp.minimum(w, pos + 1).astype(F32)
        parts.append(acc / cnt)
    pooled = jnp.concatenate(parts, axis=-1) - x
    mixed = jnp.dot(pooled.astype(BF16), wbd_ref[...], preferred_element_type=F32) * scale_ref[...]
    y_ref[...] = _rms_gain(mixed, gn_ref[...]).astype(y_ref.dtype)

    @pl.when(c == n_chunks - 1)
    def _():
        state_ref[...] = ext_ref[pre + tl - hist_len:pre + tl, :]


def _pool_mixer(z, col_blk, hist, wbd3, scale3, gn3, gn_blk, layer, row0, n_seq, seq, pos0):
    gw = wbd3.shape[-1]
    hist_len = max(POOL_WINDOWS) - 1
    tl = _pick(seq, 256)
    assert seq >= hist_len and tl >= 16 and row0 % tl == 0
    n_chunks = seq // tl
    blk0 = row0 // tl
    has_hist = hist is not None
    ins = [z] + ([hist] if has_hist else []) + [wbd3, scale3, gn3]
    specs = [pl.BlockSpec((tl, gw), lambda b, c: (blk0 + b * n_chunks + c, col_blk))]
    if has_hist:
        specs.append(pl.BlockSpec((None, None, hist_len, gw), lambda b, c: (layer, b, 0, 0)))
    specs += [
        pl.BlockSpec((None, gw, gw), lambda b, c: (layer, 0, 0)),
        pl.BlockSpec((None, 1, gw), lambda b, c: (layer, 0, 0)),
        pl.BlockSpec((None, 1, gw), lambda b, c: (layer, 0, gn_blk)),
    ]
    est = 2 * (tl * gw * 4 + gw * gw * 2 + tl * gw * 2 + 16 * gw * 4) + (tl + 16) * gw * 4 + 6 * tl * gw * 4
    return pl.pallas_call(
        functools.partial(_pool_kernel, tl=tl, n_chunks=n_chunks, has_hist=has_hist, pos0=pos0,
                          hist_len=hist_len),
        out_shape=(jax.ShapeDtypeStruct((n_seq * seq, gw), BF16),
                   jax.ShapeDtypeStruct((n_seq, hist_len, gw), F32)),
        grid=(n_seq, n_chunks),
        in_specs=specs,
        out_specs=(pl.BlockSpec((tl, gw), lambda b, c: (b * n_chunks + c, 0)),
                   pl.BlockSpec((None, hist_len, gw), lambda b, c: (b, 0, 0))),
        scratch_shapes=[pltpu.VMEM((tl + 16, gw), F32)],
        compiler_params=_params(est, ("arbitrary", "arbitrary")),
        name="pool_mixer",
    )(*ins)


def _conv_kernel(*refs, tl, n_chunks, has_hist, width):
    if has_hist:
        (a_ref, g_ref, hist_ref, wdw_ref, bdw_ref, lng_ref, lnb_ref, wpw_ref, bpw_ref, gn_ref,
         y_ref, state_ref, ext_ref, sh_ref) = refs
    else:
        (a_ref, g_ref, wdw_ref, bdw_ref, lng_ref, lnb_ref, wpw_ref, bpw_ref, gn_ref,
         y_ref, state_ref, ext_ref, sh_ref) = refs
    gw = a_ref.shape[1]
    hist_len = width - 1
    pre = 32
    c = pl.program_id(1)

    @pl.when(c == 0)
    def _():
        ext_ref[0:pre, :] = jnp.zeros((pre, gw), F32)
        if has_hist:
            ext_ref[pre - hist_len:pre, :] = hist_ref[...]

    @pl.when(c > 0)
    def _():
        ext_ref[0:pre, :] = ext_ref[tl:tl + pre, :]

    glu = a_ref[...].astype(F32) * jax.nn.sigmoid(g_ref[...].astype(F32))
    ext_ref[pre:pre + tl, :] = glu
    span = tl + pre - 8
    for r in range(1, 8):
        sh_ref[r - 1, 0:span, :] = ext_ref[r:r + span, :]
    acc = jnp.zeros((tl, gw), F32) + bdw_ref[...]
    base = pre - hist_len
    for j in range(width):
        r = (base + j) % 8
        a8 = base + j - r
        rows = ext_ref[a8:a8 + tl, :] if r == 0 else sh_ref[r - 1, a8:a8 + tl, :]
        acc = acc + rows * wdw_ref[j:j + 1, :]
    y = _layer_norm(acc, lng_ref[...], lnb_ref[...])
    y = y * jax.nn.sigmoid(y)
    out = jnp.dot(y.astype(BF16), wpw_ref[...], preferred_element_type=F32) + bpw_ref[...]
    y_ref[...] = _rms_gain(out, gn_ref[...]).astype(y_ref.dtype)

    @pl.when(c == n_chunks - 1)
    def _():
        state_ref[...] = ext_ref[pre + tl - hist_len:pre + tl, :]


def _conv_mixer(z, col_blk, hist, wdw, bdw3, lng3, lnb3, wpw3, bpw3, gn3, gn_blk, layer, row0, n_seq, seq):
    gw = wpw3.shape[-1]
    width = wdw.shape[1]
    hist_len = width - 1
    tl = _pick(seq, 256)
    assert seq >= hist_len and tl >= 32 and row0 % tl == 0
    n_chunks = seq // tl
    blk0 = row0 // tl
    has_hist = hist is not None
    vec = lambda blk=0: pl.BlockSpec((None, 1, gw), lambda b, c: (layer, 0, blk))
    ins = [z, z] + ([hist] if has_hist else []) + [wdw, bdw3, lng3, lnb3, wpw3, bpw3, gn3]
    specs = [pl.BlockSpec((tl, gw), lambda b, c: (blk0 + b * n_chunks + c, col_blk)),
             pl.BlockSpec((tl, gw), lambda b, c: (blk0 + b * n_chunks + c, col_blk + 1))]
    if has_hist:
        specs.append(pl.BlockSpec((None, None, hist_len, gw), lambda b, c: (layer, b, 0, 0)))
    specs += [pl.BlockSpec((None, width, gw), lambda b, c: (layer, 0, 0)), vec(), vec(), vec(),
              pl.BlockSpec((None, gw, gw), lambda b, c: (layer, 0, 0)), vec(), vec(gn_blk)]
    est = 2 * (2 * tl * gw * 4 + gw * gw * 2 + tl * gw * 2 + 64 * gw * 4) + 8 * (tl + 32) * gw * 4 + 8 * tl * gw * 4
    return pl.pallas_call(
        functools.partial(_conv_kernel, tl=tl, n_chunks=n_chunks, has_hist=has_hist, width=width),
        out_shape=(jax.ShapeDtypeStruct((n_seq * seq, gw), BF16),
                   jax.ShapeDtypeStruct((n_seq, hist_len, gw), F32)),
        grid=(n_seq, n_chunks),
        in_specs=specs,
        out_specs=(pl.BlockSpec((tl, gw), lambda b, c: (b * n_chunks + c, 0)),
                   pl.BlockSpec((None, hist_len, gw), lambda b, c: (b, 0, 0))),
        scratch_shapes=[pltpu.VMEM((tl + 32, gw), F32), pltpu.VMEM((7, tl + 32, gw), F32)],
        compiler_params=_params(est, ("arbitrary", "arbitrary")),
        name="conv_mixer",
    )(*ins)


def _gmlp_kernel(*refs, tl, lc, heads, emit_vn):
    if emit_vn:
        u_ref, v_ref, lng_ref, lnb_ref, ws_ref, bias_ref, gn_ref, y_ref, vn_ref = refs
    else:
        u_ref, v_ref, lng_ref, lnb_ref, ws_ref, bias_ref, gn_ref, y_ref = refs
    vn = _layer_norm(v_ref[...].astype(F32), lng_ref[...], lnb_ref[...])
    if emit_vn:
        vn_ref[...] = vn
    vb = vn.astype(BF16)
    rows = []
    for ch in range(tl // lc):
        rs = slice(ch * lc, (ch + 1) * lc)
        parts = [jnp.dot(ws_ref[h], vb[rs, h * HEAD_DIM:(h + 1) * HEAD_DIM], preferred_element_type=F32)
                 for h in range(heads)]
        rows.append(jnp.concatenate(parts, axis=-1) + bias_ref[...])
    mix = rows[0] if len(rows) == 1 else jnp.concatenate(rows, axis=0)
    y = u_ref[...].astype(F32) * mix
    y_ref[...] = _rms_gain(y, gn_ref[...]).astype(y_ref.dtype)


def _gmlp_mixer(z, col_blk, lng3, lnb3, ws4, bias3, gn3, gn_blk, layer, row0, n_rows, lc, emit_vn):
    gw = lng3.shape[-1]
    heads = gw // HEAD_DIM
    tl = lc * max(1, min(256 // lc, n_rows // lc))
    while n_rows % tl or row0 % tl:
        tl -= lc
    blk0 = row0 // tl
    vec = lambda blk=0: pl.BlockSpec((None, 1, gw), lambda i: (layer, 0, blk))
    out_shape = [jax.ShapeDtypeStruct((n_rows, gw), BF16)]
    out_specs = [pl.BlockSpec((tl, gw), lambda i: (i, 0))]
    if emit_vn:
        out_shape.append(jax.ShapeDtypeStruct((n_rows, gw), F32))
        out_specs.append(pl.BlockSpec((tl, gw), lambda i: (i, 0)))
    est = 2 * (2 * tl * gw * 4 + heads * lc * lc * 2 + lc * gw * 4 + tl * gw * 6) + 6 * tl * gw * 4
    outs = pl.pallas_call(
        functools.partial(_gmlp_kernel, tl=tl, lc=lc, heads=heads, emit_vn=emit_vn),
        out_shape=tuple(out_shape),
        grid=(n_rows // tl,),
        in_specs=[pl.BlockSpec((tl, gw), lambda i: (blk0 + i, col_blk)),
                  pl.BlockSpec((tl, gw), lambda i: (blk0 + i, col_blk + 1)),
                  vec(), vec(),
                  pl.BlockSpec((None, heads, lc, lc), lambda i: (layer, 0, 0, 0)),
                  pl.BlockSpec((None, lc, gw), lambda i: (layer, 0, 0)),
                  vec(gn_blk)],
        out_specs=tuple(out_specs),
        compiler_params=_params(est, ("arbitrary",)),
        name="gmlp_mixer",
    )(z, z, lng3, lnb3, ws4, bias3, gn3)
    return outs


def _memattn_kernel(q_ref, k_ref, v_ref, o_ref, *, heads, scale):
    parts = []
    for h in range(heads):
        sl = slice(h * HEAD_DIM, (h + 1) * HEAD_DIM)
        k = k_ref[:, sl].astype(BF16)
        v = v_ref[:, sl].astype(BF16)
        s = lax.dot_general(q_ref[:, sl], k, (((1,), (1,)), ((), ())), preferred_element_type=F32) * scale
        e = jnp.exp(s - jnp.max(s, axis=-1, keepdims=True))
        p = e / jnp.sum(e, axis=-1, keepdims=True)
        parts.append(jnp.dot(p.astype(BF16), v, preferred_element_type=F32))
    o_ref[...] = jnp.concatenate(parts, axis=-1).astype(o_ref.dtype)


def _mem_attention(q16, mem_k, mem_v, n_mem, row0, n_rows, rows_per_seq, kv_blk0):
    mw = q16.shape[1]
    heads = mw // HEAD_DIM
    tm = _pick(rows_per_seq, 256)
    per = rows_per_seq // tm
    blk0 = row0 // tm
    assert row0 % tm == 0
    est = 2 * (tm * mw * 2 + 2 * n_mem * mw * 4 + tm * mw * 2) + 8 * tm * n_mem * 4
    return pl.pallas_call(
        functools.partial(_memattn_kernel, heads=heads, scale=HEAD_DIM ** -0.5),
        out_shape=jax.ShapeDtypeStruct((n_rows, mw), BF16),
        grid=(n_rows // tm,),
        in_specs=[pl.BlockSpec((tm, mw), lambda i: (blk0 + i, 0)),
                  pl.BlockSpec((n_mem, mw), lambda i: (kv_blk0 + i // per, 0)),
                  pl.BlockSpec((n_mem, mw), lambda i: (kv_blk0 + i // per, 0))],
        out_specs=pl.BlockSpec((tm, mw), lambda i: (i, 0)),
        compiler_params=_params(est, ("arbitrary",)),
        name="mem_attention",
    )(q16, mem_k, mem_v)


def _router_kernel(x_ref, w_ref, b_ref, tril_ref, idx_ref, gate_ref, rank_ref, cnt_ref, carry_ref, *,
                   n_experts):
    @pl.when(pl.program_id(0) == 0)
    def _():
        carry_ref[...] = jnp.zeros(carry_ref.shape, F32)

    tm = x_ref.shape[0]
    logits = jnp.dot(x_ref[...], w_ref[...], preferred_element_type=F32) + b_ref[...]
    lane = lax.broadcasted_iota(I32, (tm, LANES), 1)
    lane_f = lane.astype(F32)
    work = jnp.where(lane < n_experts, logits, -jnp.inf)
    vals, sels = [], []
    idx_out = jnp.zeros((tm, LANES), I32)
    for k in range(TOP_K):
        m = jnp.max(work, axis=-1, keepdims=True)
        idx = jnp.min(jnp.where(work == m, lane_f, float(LANES)), axis=-1, keepdims=True)
        sel = lane_f == idx
        vals.append(m)
        sels.append(sel)
        idx_out = jnp.where(lane == k, idx.astype(I32), idx_out)
        work = jnp.where(sel, -jnp.inf, work)
    exps = [jnp.exp(v - vals[0]) for v in vals]
    denom = exps[0]
    for e in exps[1:]:
        denom = denom + e
    gate_out = jnp.zeros((tm, LANES), F32)
    for k in range(TOP_K):
        gate_out = jnp.where(lane == k, exps[k] / denom, gate_out)
    base = carry_ref[...]
    rank_out = jnp.zeros((tm, LANES), I32)
    tril = tril_ref[...]
    for k in range(TOP_K):
        onehot = jnp.where(sels[k], 1.0, 0.0)
        before = jnp.dot(tril, onehot.astype(BF16), preferred_element_type=F32) + base
        rank = jnp.sum(onehot * before, axis=-1, keepdims=True)
        rank_out = jnp.where(lane == k, rank.astype(I32), rank_out)
        base = base + jnp.sum(onehot, axis=0, keepdims=True)
    carry_ref[...] = base
    idx_ref[...] = idx_out
    gate_ref[...] = gate_out
    rank_ref[...] = rank_out
    cnt_ref[...] = base.astype(I32)


def _router(x16, w3, b3, layer, n_experts):
    m, d = x16.shape
    tm = _pick(m, 256, 16)
    tril = jnp.tri(tm, k=-1, dtype=BF16)
    row = pl.BlockSpec((tm, LANES), lambda i: (i, 0))
    est = 2 * (tm * d * 2 + d * LANES * 2 + tm * tm * 2 + 3 * tm * LANES * 4) + 16 * tm * LANES * 4
    return pl.pallas_call(
        functools.partial(_router_kernel, n_experts=n_experts),
        out_shape=(jax.ShapeDtypeStruct((m, LANES), I32), jax.ShapeDtypeStruct((m, LANES), F32),
                   jax.ShapeDtypeStruct((m, LANES), I32), jax.ShapeDtypeStruct((1, LANES), I32)),
        grid=(m // tm,),
        in_specs=[pl.BlockSpec((tm, d), lambda i: (i, 0)),
                  pl.BlockSpec((None, d, LANES), lambda i: (layer, 0, 0)),
                  pl.BlockSpec((None, 1, LANES), lambda i: (layer, 0, 0)),
                  pl.BlockSpec((tm, tm), lambda i: (0, 0))],
        out_specs=(row, row, row, pl.BlockSpec((1, LANES), lambda i: (0, 0))),
        scratch_shapes=[pltpu.VMEM((1, LANES), F32)],
        compiler_params=_params(est, ("arbitrary",)),
        name="router",
    )(x16, w3, b3, tril)


def _gather_rows(src_hbm, idx_smem, slot, buf, sem, n, unroll=8):
    def body(j, _):
        t = idx_smem[slot, j]
        pltpu.make_async_copy(_src_row(src_hbm, t, buf), buf.at[slot, pl.ds(j, 1)], sem.at[slot]).start()
        return 0
    lax.fori_loop(0, n, body, 0, unroll=unroll)


def _src_row(src_hbm, t, buf):
    return src_hbm.at[pl.ds(t, 1)] if len(buf.shape) == 4 else src_hbm.at[t]


def _wait_rows(src_hbm, slot, buf, sem, n, unroll=8):
    def body(j, _):
        pltpu.make_async_copy(_src_row(src_hbm, 0, buf), buf.at[slot, pl.ds(j, 1)], sem.at[slot]).wait()
        return 0
    lax.fori_loop(0, n, body, 0, unroll=unroll)


def _dispatch_kernel(tok_ref, x_hbm, o_ref, idx_smem, buf, tile_ref, isem, gsem, *, tmb, n_blk):
    rb = pl.program_id(0)
    slot = rb % 2
    nxt = 1 - slot

    def idx_copy(blk, s):
        return pltpu.make_async_copy(tok_ref.at[blk], idx_smem.at[s], isem.at[s])

    @pl.when(rb == 0)
    def _():
        idx_copy(0, 0).start()
        idx_copy(0, 0).wait()
        _gather_rows(x_hbm, idx_smem, 0, buf, gsem, tmb)
        if n_blk > 1:
            idx_copy(1, 1).start()

    @pl.when(rb + 1 < n_blk)
    def _():
        idx_copy(rb + 1, nxt).wait()
        _gather_rows(x_hbm, idx_smem, nxt, buf, gsem, tmb)

    @pl.when(rb + 2 < n_blk)
    def _():
        idx_copy(rb + 2, slot).start()

    _wait_rows(x_hbm, slot, buf, gsem, tmb)
    tile_ref[...] = buf[slot].reshape(tile_ref.shape)
    o_ref[...] = tile_ref[...].astype(o_ref.dtype)


def _dispatch(row_tok2, x_rows, tmb):
    n_blk = row_tok2.shape[0]
    d = x_rows.shape[-1]
    est = 2 * tmb * d * 4 + 2 * tmb * d * 2 + row_tok2.size * 4 * 2 + 2 * tmb * d * 4
    return pl.pallas_call(
        functools.partial(_dispatch_kernel, tmb=tmb, n_blk=n_blk),
        out_shape=jax.ShapeDtypeStruct((n_blk * tmb, d), BF16),
        grid=(n_blk,),
        in_specs=[pl.BlockSpec((n_blk, tmb), lambda i: (0, 0)),
                  pl.BlockSpec(memory_space=pl.ANY)],
        out_specs=pl.BlockSpec((tmb, d), lambda i: (i, 0)),
        scratch_shapes=[pltpu.SMEM((2, tmb), I32), pltpu.VMEM((2, tmb, 1, d), F32), pltpu.VMEM((tmb, d), F32),
                        pltpu.SemaphoreType.DMA((2,)), pltpu.SemaphoreType.DMA((2,))],
        compiler_params=_params(est, ("arbitrary",)),
        name="moe_dispatch",
    )(row_tok2, x_rows)


def _expert_changed(meta_ref, rb):
    prev = meta_ref[1 + jnp.maximum(rb - 1, 0)]
    return jnp.logical_or(rb == 0, meta_ref[1 + rb] != prev)


def _expert_up_kernel(meta_ref, x_ref, wg_ref, wu_ref, bg_ref, bu_ref, h_ref, wgb_ref, wub_ref):
    rb = pl.program_id(1)
    live = rb < meta_ref[0]

    @pl.when(jnp.logical_and(live, _expert_changed(meta_ref, rb)))
    def _():
        wgb_ref[...] = wg_ref[...].astype(BF16)
        wub_ref[...] = wu_ref[...].astype(BF16)

    @pl.when(live)
    def _():
        tmb = x_ref.shape[0]
        rc = _pick(tmb, 256)
        for r in range(tmb // rc):
            rs = slice(r * rc, (r + 1) * rc)
            x = x_ref[rs, :]
            gate = jnp.dot(x, wgb_ref[...], preferred_element_type=F32) + bg_ref[...]
            up = jnp.dot(x, wub_ref[...], preferred_element_type=F32) + bu_ref[...]
            gate = jnp.minimum(gate, SWIGLU_LIMIT)
            up = jnp.clip(up, -SWIGLU_LIMIT, SWIGLU_LIMIT)
            h_ref[rs, :] = (gate * jax.nn.sigmoid(SWIGLU_ALPHA * gate) * (up + 1.0)).astype(h_ref.dtype)

    @pl.when(jnp.logical_not(live))
    def _():
        h_ref[...] = jnp.zeros(h_ref.shape, h_ref.dtype)


def _expert_up(meta, xs, w_gate_up, b_gate_up4, layer, tmb):
    n_rows, d = xs.shape
    n_blk = n_rows // tmb
    ff = w_gate_up.shape[-1] // 2
    tf = _pick(ff, 512, LANES)
    nf = ff // tf
    live_blk = lambda rb, m: jnp.minimum(rb, m[0] - 1)
    w_spec = lambda off: pl.BlockSpec((None, None, d, tf), lambda f, rb, m: (layer, m[1 + rb], 0, off + f))
    b_spec = lambda off: pl.BlockSpec((None, None, 1, tf), lambda f, rb, m: (layer, m[1 + rb], 0, off + f))
    est = 2 * (tmb * d * 2 + 2 * d * tf * 4 + tmb * tf * 2) + 2 * d * tf * 2 + 6 * tmb * tf * 4
    return pl.pallas_call(
        _expert_up_kernel,
        out_shape=jax.ShapeDtypeStruct((n_rows, ff), BF16),
        grid_spec=pltpu.PrefetchScalarGridSpec(
            num_scalar_prefetch=1,
            grid=(nf, n_blk),
            in_specs=[pl.BlockSpec((tmb, d), lambda f, rb, m: (live_blk(rb, m), 0)),
                      w_spec(0), w_spec(nf), b_spec(0), b_spec(nf)],
            out_specs=pl.BlockSpec((tmb, tf), lambda f, rb, m: (rb, f)),
            scratch_shapes=[pltpu.VMEM((d, tf), BF16), pltpu.VMEM((d, tf), BF16)]),
        compiler_params=_params(est, ("arbitrary", "arbitrary")),
        name="expert_up",
    )(meta, xs, w_gate_up, w_gate_up, b_gate_up4, b_gate_up4)


def _expert_down_kernel(meta_ref, h_ref, w_ref, b_ref, y_ref, wb_ref):
    rb = pl.program_id(1)
    live = rb < meta_ref[0]

    @pl.when(jnp.logical_and(live, _expert_changed(meta_ref, rb)))
    def _():
        wb_ref[...] = w_ref[...].astype(BF16)

    @pl.when(live)
    def _():
        tmb, _, tn = y_ref.shape
        rc = _pick(tmb, 128)
        for r in range(tmb // rc):
            rs = slice(r * rc, (r + 1) * rc)
            y = jnp.dot(h_ref[rs, :], wb_ref[...], preferred_element_type=F32) + b_ref[...]
            y_ref[rs] = y.reshape(rc, 1, tn)

    @pl.when(jnp.logical_not(live))
    def _():
        y_ref[...] = jnp.zeros(y_ref.shape, y_ref.dtype)


def _expert_down(meta, hs, w_down, b_down4, layer, tmb):
    n_rows, ff = hs.shape
    n_blk = n_rows // tmb
    d = w_down.shape[-1]
    tn = _pick(d, 2048, LANES)
    live_blk = lambda rb, m: jnp.minimum(rb, m[0] - 1)
    est = 2 * (tmb * ff * 2 + ff * tn * 4 + tmb * tn * 4) + ff * tn * 2 + 2 * tmb * tn * 4
    return pl.pallas_call(
        _expert_down_kernel,
        out_shape=jax.ShapeDtypeStruct((n_rows, 1, d), F32),
        grid_spec=pltpu.PrefetchScalarGridSpec(
            num_scalar_prefetch=1,
            grid=(d // tn, n_blk),
            in_specs=[pl.BlockSpec((tmb, ff), lambda c, rb, m: (live_blk(rb, m), 0)),
                      pl.BlockSpec((None, None, ff, tn), lambda c, rb, m: (layer, m[1 + rb], 0, c)),
                      pl.BlockSpec((None, None, 1, tn), lambda c, rb, m: (layer, m[1 + rb], 0, c))],
            out_specs=pl.BlockSpec((tmb, 1, tn), lambda c, rb, m: (rb, 0, c)),
            scratch_shapes=[pltpu.VMEM((ff, tn), BF16)]),
        compiler_params=_params(est, ("arbitrary", "arbitrary")),
        name="expert_down",
    )(meta, hs, w_down, b_down4)


def _combine_kernel(dest_ref, y_hbm, gate_ref, h_ref, g_ref, b_ref, o32_ref, o16_ref,
                    idx_smem, buf, isem, gsem, *, tmc, n_blk, alpha):
    i = pl.program_id(0)
    slot = i % 2
    nxt = 1 - slot
    n = TOP_K * tmc

    def idx_copy(blk, s):
        return pltpu.make_async_copy(dest_ref.at[blk], idx_smem.at[s], isem.at[s])

    @pl.when(i == 0)
    def _():
        idx_copy(0, 0).start()
        idx_copy(0, 0).wait()
        _gather_rows(y_hbm, idx_smem, 0, buf, gsem, n)
        if n_blk > 1:
            idx_copy(1, 1).start()

    @pl.when(i + 1 < n_blk)
    def _():
        idx_copy(i + 1, nxt).wait()
        _gather_rows(y_hbm, idx_smem, nxt, buf, gsem, n)

    @pl.when(i + 2 < n_blk)
    def _():
        idx_copy(i + 2, slot).start()

    _wait_rows(y_hbm, slot, buf, gsem, n)
    gates = gate_ref[...]
    mix = None
    for k in range(TOP_K):
        part = gates[:, k:k + 1] * buf[slot, k * tmc:(k + 1) * tmc, :]
        mix = part if mix is None else mix + part
    out = _layer_norm(alpha * h_ref[...] + mix, g_ref[...], b_ref[...])
    o32_ref[...] = out
    o16_ref[...] = out.astype(BF16)


def _combine_ln(dest2, y, gates, h32, g3, b3, layer, alpha, tmc):
    m, d = h32.shape
    n_blk = m // tmc
    n = TOP_K * tmc
    row = pl.BlockSpec((tmc, d), lambda i: (i, 0))
    par = pl.BlockSpec((None, 1, d), lambda i: (layer, 0, 0))
    est = 2 * n * d * 4 + 2 * (tmc * d * 10 + tmc * LANES * 4) + dest2.size * 4 * 2 + 4 * tmc * d * 4
    return pl.pallas_call(
        functools.partial(_combine_kernel, tmc=tmc, n_blk=n_blk, alpha=alpha),
        out_shape=(jax.ShapeDtypeStruct((m, d), F32), jax.ShapeDtypeStruct((m, d), BF16)),
        grid=(n_blk,),
        in_specs=[pl.BlockSpec((n_blk, n), lambda i: (0, 0)),
                  pl.BlockSpec(memory_space=pl.ANY),
                  pl.BlockSpec((tmc, LANES), lambda i: (i, 0)),
                  row, par, par],
        out_specs=(row, row),
        scratch_shapes=[pltpu.SMEM((2, n), I32), pltpu.VMEM((2, n, d), F32),
                        pltpu.SemaphoreType.DMA((2,)), pltpu.SemaphoreType.DMA((2,))],
        compiler_params=_params(est, ("arbitrary",)),
        name="moe_combine_ln",
    )(dest2, y, gates, h32, g3, b3)


def _route_tables(idx, rank, counts, tmb, n_blk, tmc):
    t = idx.shape[0]
    n_exp = counts.shape[0]
    padded = (counts + tmb - 1) // tmb * tmb
    pad_end = jnp.cumsum(padded)
    pad_start = pad_end - padded
    onehot = idx[..., None] == jnp.arange(n_exp, dtype=I32)
    dest = rank + jnp.sum(jnp.where(onehot, pad_start, 0), axis=-1)
    n_used = pad_end[-1] // tmb
    blk = jnp.arange(n_blk, dtype=I32)
    blk_e = jnp.minimum(jnp.sum(pad_end[None, :] <= (blk * tmb)[:, None], axis=1), n_exp - 1).astype(I32)
    blk_e = jnp.where(blk < n_used, blk_e, blk_e[jnp.maximum(n_used - 1, 0)])
    meta = jnp.concatenate([n_used[None].astype(I32), blk_e])
    tok = jnp.broadcast_to(jnp.arange(t, dtype=I32)[:, None], dest.shape)
    row_tok = jnp.zeros((n_blk * tmb,), I32).at[dest.reshape(-1)].set(tok.reshape(-1), unique_indices=True)
    dest2 = dest.reshape(t // tmc, tmc, TOP_K).transpose(0, 2, 1).reshape(t // tmc, TOP_K * tmc)
    return meta, row_tok.reshape(n_blk, tmb), dest2


def kernel(x_prompt, x_sample, mem_prompt, cache_fox_k, cache_fox_v, cache_fox_logf, cache_pool, cache_conv,
           cache_mem_k, cache_mem_v, emb_ln_g, emb_ln_b, w_in, b_fgate, w_pool, pool_scale, w_dw, b_dw,
           conv_ln_g, conv_ln_b, w_pw, b_pw, gmlp_ln_g, gmlp_ln_b, w_spatial, b_spatial, group_norm_g, w_out,
           ln1_g, ln1_b, w_xq, w_xk, w_xv, w_xo, ln2_g, ln2_b, w_router, b_router, w_gate_up, b_gate_up,
           w_down, b_down, ln3_g, ln3_b):
    bp, seq, d = x_prompt.shape
    bs, ls, _ = x_sample.shape
    depth = w_in.shape[0]
    gw = d // 4
    heads = gw // HEAD_DIM
    past = cache_fox_k.shape[2]
    n_mem = mem_prompt.shape[1]
    mw = w_xq.shape[-1]
    n_exp = w_router.shape[-1]
    tp, ts = bp * seq, bs * ls
    t = tp + ts
    alpha = float((2 * depth) ** 0.25)
    lc_p = min(w_spatial.shape[-1], seq)
    lc_s = min(w_spatial.shape[-1], ls)
    tq = _pick(seq, 256)
    nq = seq // tq

    vec3 = lambda a: a.reshape(depth, 1, a.shape[-1])
    rest0 = 3 * gw + heads
    w_qkv16 = w_in[:, :, :3 * gw].astype(BF16)
    w_rest16 = w_in[:, :, rest0:].astype(BF16)
    w_out16 = w_out.astype(BF16)
    w_f = jnp.pad(w_in[:, :, 3 * gw:rest0], ((0, 0), (0, 0), (0, LANES - heads)))
    b_f = jnp.pad(b_fgate, ((0, 0), (0, LANES - heads))).reshape(depth, 1, LANES)
    n_win = len(POOL_WINDOWS)
    pg = gw // n_win
    wbd = jnp.zeros((depth, gw, gw), F32)
    for g in range(n_win):
        wbd = wbd.at[:, g * pg:(g + 1) * pg, g * pg:(g + 1) * pg].set(w_pool[:, g])
    wbd16 = wbd.astype(BF16)
    wpw16 = w_pw.astype(BF16)

    def spatial(lc):
        causal = jnp.tril(jnp.ones((lc, lc), dtype=bool))
        ws = jnp.where(causal[None, None], w_spatial[:, :, :lc, :lc], 0).astype(BF16)
        bias = jnp.repeat(jnp.transpose(b_spatial[:, :, :lc], (0, 2, 1)), HEAD_DIM, axis=-1)
        return ws, bias

    ws_p, bias_p = spatial(lc_p)
    ws_s, bias_s = spatial(lc_s)
    w_router16 = jnp.pad(w_router, ((0, 0), (0, 0), (0, LANES - n_exp))).astype(BF16)
    b_router3 = jnp.pad(b_router, ((0, 0), (0, LANES - n_exp))).reshape(depth, 1, LANES)
    b_gu4 = b_gate_up.reshape(depth, n_exp, 1, b_gate_up.shape[-1])
    b_dn4 = b_down.reshape(depth, n_exp, 1, d)
    gn3 = vec3(group_norm_g)
    cache_k4 = cache_fox_k.reshape(depth, bs, past, gw)
    cache_v4 = cache_fox_v.reshape(depth, bs, past, gw)
    cache_mk = cache_mem_k.reshape(depth * bs * n_mem, mw)
    cache_mv = cache_mem_v.reshape(depth * bs * n_mem, mw)
    mem16 = mem_prompt.reshape(bp * n_mem, d).astype(BF16)

    tmb = 512 if t >= 4096 else 64
    n_blk = -(-(t * TOP_K + n_exp * (tmb - 1)) // tmb)
    tmc = _pick(t, 64)

    x_all = jnp.concatenate([x_prompt.reshape(tp, d), x_sample.reshape(ts, d)], axis=0)
    x32, x16 = _add_ln(x_all, None, emb_ln_g.reshape(1, 1, d), emb_ln_b.reshape(1, 1, d), 0, alpha)

    outs = [[] for _ in range(13)]
    for l in range(depth):
        (q16,) = _dense_mm([x16], w_qkv16, l, 0, gw, (BF16,), tn_target=1024)
        kv32, kv16 = _dense_mm([x16], w_qkv16, l, gw, 2 * gw, (F32, BF16), tn_target=1024)
        (zr,) = _dense_mm([x16], w_rest16, l, 0, 5 * gw, (BF16,), tn_target=1024)
        (logf_pad,) = _dense_mm([x16], w_f, l, 0, LANES, (F32,), bias3=b_f, epilogue="log_sigmoid")
        logf = logf_pad[:, :heads]

        lf_p = logf[:tp].reshape(bp, seq, heads)
        lf_s = logf[tp:].reshape(bs, ls, heads)
        cum_parts = _cumsum_time(jnp.transpose(lf_p, (1, 0, 2)).reshape(seq, bp * heads),
                                 split_scale=HEAD_DIM ** 0.5)
        hi, mid, lo = (jnp.transpose(c.reshape(seq, bp, heads), (1, 0, 2)).astype(F32)[..., None]
                       for c in cum_parts[1:])
        lane = jnp.arange(HEAD_DIM)
        place = lambda a, b, c, off: (jnp.where(lane == off, a, 0.0) + jnp.where(lane == off + 1, b, 0.0)
                                      + jnp.where(lane == off + 2, c, 0.0))
        ones_at = lambda off: ((lane >= off) & (lane < off + 3)).astype(F32)
        aug_q = (place(hi, mid, lo, 0) + ones_at(3)).astype(BF16).reshape(tp, gw)
        aug_k = (ones_at(0) - place(hi, mid, lo, 3)).astype(BF16).reshape(tp, gw)
        lk = past + ls
        lkp = -(-lk // 256) * 256
        lf_all = jnp.concatenate([cache_fox_logf[l].astype(F32), lf_s], axis=1)
        lf_all = jnp.pad(jnp.transpose(lf_all, (1, 0, 2)).reshape(lk, bs * heads), ((0, lkp - lk), (0, 0)))
        cum_s = jnp.transpose(_cumsum_time(lf_all)[0][:lk].reshape(lk, bs, heads), (1, 0, 2))
        cq_s = cum_s[:, past:].reshape(ts, heads)
        ck_s = jnp.transpose(cum_s, (0, 2, 1))
        tkc = _pick(past, 512)
        ckc = ck_s[:, :, :past].reshape(bs, heads, past // tkc, tkc)
        ckn = ck_s[:, :, past:]

        ya_p = _fox_prompt(q16, kv16, aug_q, aug_k, gn3, l, bp, seq, tq)
        ya_s = _fox_sample(q16, kv16, cache_k4, cache_v4, cq_s, ckc, ckn, gn3, l, tp, bs, ls)
        yb_p, pool_p = _pool_mixer(zr, 0, None, wbd16, vec3(pool_scale), gn3, 1, l, 0, bp, seq, 0)
        yb_s, pool_s = _pool_mixer(zr, 0, cache_pool, wbd16, vec3(pool_scale), gn3, 1, l, tp, bs, ls, past)
        conv_args = (w_dw, vec3(b_dw), vec3(conv_ln_g), vec3(conv_ln_b), wpw16, vec3(b_pw), gn3, 2, l)
        yc_p, conv_p = _conv_mixer(zr, 1, None, *conv_args, 0, bp, seq)
        yc_s, conv_s = _conv_mixer(zr, 1, cache_conv, *conv_args, tp, bs, ls)
        (yd_p,) = _gmlp_mixer(zr, 3, vec3(gmlp_ln_g), vec3(gmlp_ln_b), ws_p, bias_p, gn3, 3, l, 0, tp, lc_p, False)
        yd_s, gv_s = _gmlp_mixer(zr, 3, vec3(gmlp_ln_g), vec3(gmlp_ln_b), ws_s, bias_s, gn3, 3, l, tp, ts, lc_s, True)
        (mix,) = _dense_mm([(ya_p, ya_s), (yb_p, yb_s), (yc_p, yc_s), (yd_p, yd_s)], w_out16, l, 0, d, (F32,),
                           tn_target=1024)
        h1_32, h1_16 = _add_ln(x32, mix, vec3(ln1_g), vec3(ln1_b), l, alpha)
        (qm16,) = _dense_mm([h1_16], w_xq, l, 0, mw, (BF16,))
        (mk,) = _dense_mm([mem16], w_xk, l, 0, mw, (F32,))
        (mv,) = _dense_mm([mem16], w_xv, l, 0, mw, (F32,))
        om_p = _mem_attention(qm16, mk, mv, n_mem, 0, tp, seq, 0)
        om_s = _mem_attention(qm16, cache_mk, cache_mv, n_mem, tp, ts, ls, l * bs)
        (xo,) = _dense_mm([(om_p, om_s)], w_xo, l, 0, d, (F32,), tn_target=1024)
        h2_32, h2_16, h2_rows = _add_ln(h1_32, xo, vec3(ln2_g), vec3(ln2_b), l, alpha, emit_rows=True)

        idx_pad, gates, rank_pad, cnt_pad = _router(h2_16, w_router16, b_router3, l, n_exp)
        meta, row_tok2, dest2 = _route_tables(idx_pad[:, :TOP_K], rank_pad[:, :TOP_K], cnt_pad[0, :n_exp],
                                              tmb, n_blk, tmc)
        xs = _dispatch(row_tok2, h2_rows, tmb)
        hs = _expert_up(meta, xs, w_gate_up, b_gu4, l, tmb)
        y = _expert_down(meta, hs, w_down, b_dn4, l, tmb)
        x32, x16 = _combine_ln(dest2, y, gates, h2_32, vec3(ln3_g), vec3(ln3_b), l, alpha, tmc)

        k32, v32 = kv32[:, :gw], kv32[:, gw:]
        per_layer = (
            k32[:tp].reshape(bp, seq, heads, HEAD_DIM), v32[:tp].reshape(bp, seq, heads, HEAD_DIM), lf_p,
            pool_p, conv_p, mk.reshape(bp, n_mem, mw // HEAD_DIM, HEAD_DIM),
            mv.reshape(bp, n_mem, mw // HEAD_DIM, HEAD_DIM),
            k32[tp:].reshape(bs, ls, heads, HEAD_DIM), v32[tp:].reshape(bs, ls, heads, HEAD_DIM), lf_s,
            pool_s, conv_s, gv_s.reshape(bs, ls, heads, HEAD_DIM))
        for acc, val in zip(outs, per_layer):
            acc.append(val)

    return (x32[:tp].reshape(bp, seq, d), x32[tp:].reshape(bs, ls, d)) + tuple(jnp.stack(o) for o in outs)
```

```python
import functools
import math

import jax
import jax.numpy as jnp
from jax import lax
from jax.experimental import pallas as pl
from jax.experimental.pallas import tpu as pltpu

F32 = jnp.float32
BF16 = jnp.bfloat16
I32 = jnp.int32

HEAD_DIM = 128
LANES = 128
LN_EPS = 1e-5
NEG_INF = -1e30
POOL_WINDOWS = (2, 4, 8, 16)
TOP_K = 4
SWIGLU_LIMIT = 7.0
SWIGLU_ALPHA = 1.702
VMEM_CAP = 60 * 1024 * 1024


def _pick(n, target, mult=8):
    if n <= target:
        return n
    t = target - target % mult
    while t >= mult:
        if n % t == 0:
            return t
        t -= mult
    return n


def _params(est_bytes, semantics):
    limit = int(min(VMEM_CAP, max(16 * 1024 * 1024, est_bytes * 5 // 4 + (2 << 20))))
    return pltpu.CompilerParams(dimension_semantics=semantics, vmem_limit_bytes=limit)


def _layer_norm(x, g, b):
    mu = jnp.mean(x, axis=-1, keepdims=True)
    xc = x - mu
    var = jnp.mean(xc * xc, axis=-1, keepdims=True)
    return xc * lax.rsqrt(var + LN_EPS) * g + b


def _rms_gain(y, g):
    return y * lax.rsqrt(jnp.mean(y * y, axis=-1, keepdims=True) + LN_EPS) * g


def _log_sigmoid(x):
    return -(jnp.maximum(-x, 0.0) + jnp.log1p(jnp.exp(-jnp.abs(x))))


def _add_ln_kernel(*refs, alpha, has_res, emit_rows):
    n_in = 4 if has_res else 3
    ins, outs = refs[:n_in], refs[n_in:]
    if has_res:
        x_ref, y_ref, g_ref, b_ref = ins
        x = alpha * x_ref[...] + y_ref[...].astype(F32)
    else:
        x_ref, g_ref, b_ref = ins
        x = x_ref[...]
    out = _layer_norm(x, g_ref[...], b_ref[...])
    outs[0][...] = out
    outs[1][...] = out.astype(BF16)
    if emit_rows:
        outs[2][...] = out.reshape(outs[2].shape)


def _add_ln(x, y, g3, b3, layer, alpha, emit_rows=False):
    m, d = x.shape
    tm = _pick(m, 256)
    row = pl.BlockSpec((tm, d), lambda i: (i, 0))
    par = pl.BlockSpec((None, 1, d), lambda i: (layer, 0, 0))
    ins = [x] + ([y] if y is not None else []) + [g3, b3]
    specs = [row] + ([row] if y is not None else []) + [par, par]
    out_shape = [jax.ShapeDtypeStruct((m, d), F32), jax.ShapeDtypeStruct((m, d), BF16)]
    out_specs = [row, row]
    if emit_rows:
        out_shape.append(jax.ShapeDtypeStruct((m, 1, d), F32))
        out_specs.append(pl.BlockSpec((tm, 1, d), lambda i: (i, 0, 0)))
    est = tm * d * (4 * (2 if y is not None else 1) + 4 + 2 + (4 if emit_rows else 0)) * 2 + tm * d * 8
    return pl.pallas_call(
        functools.partial(_add_ln_kernel, alpha=alpha, has_res=y is not None, emit_rows=emit_rows),
        out_shape=tuple(out_shape),
        grid=(m // tm,),
        in_specs=specs,
        out_specs=tuple(out_specs),
        compiler_params=_params(est, ("arbitrary",)),
        name="add_ln",
    )(*ins)


def _mm_kernel(*refs, ks, cast_w, has_bias, epilogue, n_out, n_head_blk):
    n_a = len(ks)
    n_groups = 2 if n_head_blk is not None else 1
    a_groups = [refs[g * n_a:(g + 1) * n_a] for g in range(n_groups)]
    w_ref = refs[n_groups * n_a]
    pos = n_groups * n_a + 1
    bias_ref = None
    if has_bias:
        bias_ref = refs[pos]
        pos += 1
    o_refs = refs[pos:pos + n_out]
    if cast_w:
        wb_ref = refs[pos + n_out]

        @pl.when(pl.program_id(1) == 0)
        def _():
            wb_ref[...] = w_ref[...].astype(BF16)
        w = wb_ref
    else:
        w = w_ref

    def run(a_refs):
        acc = None
        off = 0
        for a_ref, k in zip(a_refs, ks):
            part = jnp.dot(a_ref[...], w[off:off + k, :], preferred_element_type=F32)
            acc = part if acc is None else acc + part
            off += k
        if has_bias:
            acc = acc + bias_ref[...]
        if epilogue == "log_sigmoid":
            acc = _log_sigmoid(acc)
        for o_ref in o_refs:
            o_ref[...] = acc.astype(o_ref.dtype)

    if n_head_blk is None:
        run(a_groups[0])
    else:
        i = pl.program_id(1)
        pl.when(i < n_head_blk)(lambda: run(a_groups[0]))
        pl.when(i >= n_head_blk)(lambda: run(a_groups[1]))


def _dense_mm(a_list, w3, layer, col_off, n, out_dtypes, *, bias3=None, epilogue=None,
              tm_target=512, tn_target=512):
    split = isinstance(a_list[0], tuple)
    heads_ = [a[0] if split else a for a in a_list]
    ks = tuple(a.shape[1] for a in heads_)
    k = sum(ks)
    cast_w = w3.dtype != BF16
    if split:
        m_head, m_tail = a_list[0][0].shape[0], a_list[0][1].shape[0]
        m = m_head + m_tail
        tm = _pick(math.gcd(m_head, m_tail), tm_target, 16)
        n_head_blk = m_head // tm
        last_tail = m_tail // tm - 1
        a_specs = ([pl.BlockSpec((tm, kg), lambda j, i: (jnp.minimum(i, n_head_blk - 1), 0)) for kg in ks]
                   + [pl.BlockSpec((tm, kg), lambda j, i: (jnp.clip(i - n_head_blk, 0, last_tail), 0)) for kg in ks])
        ins = [a[0] for a in a_list] + [a[1] for a in a_list] + [w3]
    else:
        m = heads_[0].shape[0]
        tm = _pick(m, tm_target, 16)
        n_head_blk = None
        a_specs = [pl.BlockSpec((tm, kg), lambda j, i: (i, 0)) for kg in ks]
        ins = list(a_list) + [w3]
    tn = _pick(math.gcd(n, col_off), tn_target, LANES)
    assert col_off % tn == 0 and n % tn == 0
    off_blk = col_off // tn
    w_spec = pl.BlockSpec((None, k, tn), lambda j, i: (layer, 0, off_blk + j))
    specs = a_specs + [w_spec]
    if bias3 is not None:
        ins.append(bias3)
        specs.append(pl.BlockSpec((None, 1, tn), lambda j, i: (layer, 0, off_blk + j)))
    out_spec = pl.BlockSpec((tm, tn), lambda j, i: (i, j))
    out_shape = tuple(jax.ShapeDtypeStruct((m, n), dt) for dt in out_dtypes)
    scratch = [pltpu.VMEM((k, tn), BF16)] if cast_w else []
    est = ((4 if split else 2) * tm * k * 2 + 2 * k * tn * w3.dtype.itemsize + (k * tn * 2 if cast_w else 0)
           + sum(2 * tm * tn * jnp.dtype(dt).itemsize for dt in out_dtypes) + 2 * tm * tn * 4)
    outs = pl.pallas_call(
        functools.partial(_mm_kernel, ks=ks, cast_w=cast_w, has_bias=bias3 is not None,
                          epilogue=epilogue, n_out=len(out_dtypes), n_head_blk=n_head_blk),
        out_shape=out_shape,
        grid=(n // tn, m // tm),
        in_specs=specs,
        out_specs=tuple(out_spec for _ in out_dtypes),
        scratch_shapes=scratch,
        compiler_params=_params(est, ("arbitrary", "arbitrary")),
        name="dense_mm",
    )(*ins)
    return outs


def _split3(x):
    hi = x.astype(BF16)
    r1 = x - hi.astype(F32)
    mid = r1.astype(BF16)
    lo = (r1 - mid.astype(F32)).astype(BF16)
    return hi, mid, lo


def _cumsum_kernel(x_ref, tri_ref, o_ref, *split_refs, blk, n_blk, split_scale):
    tri = tri_ref[...]
    carry = jnp.zeros((1, x_ref.shape[1]), F32)
    for r in range(n_blk):
        rs = slice(r * blk, (r + 1) * blk)
        hi, mid, lo = _split3(x_ref[rs, :])
        c = (jnp.dot(tri, hi, preferred_element_type=F32)
             + jnp.dot(tri, mid, preferred_element_type=F32)
             + jnp.dot(tri, lo, preferred_element_type=F32)) + carry
        o_ref[rs, :] = c
        if split_refs:
            for ref, term in zip(split_refs, _split3(c * split_scale)):
                ref[rs, :] = term
        carry = c[blk - 1:blk, :]


def _cumsum_time(x, split_scale=None):
    lp, c = x.shape
    blk = _pick(lp, 256)
    tri = jnp.tri(blk, dtype=BF16)
    full = lambda shp: pl.BlockSpec(shp, lambda i: (0,) * len(shp))
    n_split = 3 if split_scale is not None else 0
    out_shape = (jax.ShapeDtypeStruct((lp, c), F32),) + (jax.ShapeDtypeStruct((lp, c), BF16),) * n_split
    return pl.pallas_call(
        functools.partial(_cumsum_kernel, blk=blk, n_blk=lp // blk, split_scale=split_scale),
        out_shape=out_shape,
        grid=(1,),
        in_specs=[full((lp, c)), full((blk, blk))],
        out_specs=tuple(full((lp, c)) for _ in out_shape),
        compiler_params=_params(6 * lp * LANES * 4 * 2, ("arbitrary",)),
        name="cumsum_time",
    )(x, tri)


def _softmax_step(q, k, v, cq, ck, carry, scale, mask):
    m, l, acc = carry
    s = lax.dot_general(q, k, (((1,), (1,)), ((), ())), preferred_element_type=F32) * scale
    s = s + cq - ck
    if mask is not None:
        s = jnp.where(mask, s, NEG_INF)
    m_new = jnp.maximum(m, jnp.max(s, axis=-1, keepdims=True))
    p = jnp.exp(s - m_new)
    a = jnp.exp(m - m_new)
    l_new = a * l + jnp.sum(p, axis=-1, keepdims=True)
    acc_new = a * acc + jnp.dot(p.astype(BF16), v, preferred_element_type=F32)
    return m_new, l_new, acc_new


def _fox_prompt_kernel(q_ref, aq_ref, k_ref, ak_ref, v_ref, gn_ref, y_ref, vt_ref, m_ref, l_ref, acct_ref,
                       acc_ref, s_ref, p_ref, *, tq, heads, c2):
    qi = pl.program_id(1)
    nk = k_ref.shape[0] // tq

    @pl.when(qi == 0)
    def _():
        for h in range(heads):
            sl = slice(h * HEAD_DIM, (h + 1) * HEAD_DIM)
            for j in range(nk):
                vt_ref[h, j] = v_ref[j * tq:(j + 1) * tq, sl].T

    m_ref[...] = jnp.full(m_ref.shape, NEG_INF, F32)
    l_ref[...] = jnp.zeros(l_ref.shape, F32)
    acct_ref[...] = jnp.zeros(acct_ref.shape, F32)

    def key_block(j, mask):
        ks = pl.multiple_of(j * tq, tq)
        for h in range(heads):
            sl = slice(h * HEAD_DIM, (h + 1) * HEAD_DIM)
            k2 = jnp.concatenate([k_ref[pl.ds(ks, tq), sl], ak_ref[pl.ds(ks, tq), sl]], axis=1)
            q2 = jnp.concatenate([q_ref[:, sl], aq_ref[:, sl]], axis=1)
            s_ref[h] = lax.dot_general(k2, q2, (((1,), (1,)), ((), ())), preferred_element_type=F32)
        for h in range(heads):
            s = s_ref[h] * c2
            if mask is not None:
                s = jnp.where(mask, s, NEG_INF)
            m_old = m_ref[h]
            m_new = jnp.maximum(m_old, jnp.max(s, axis=0, keepdims=True))
            p = jnp.exp2(s - m_new)
            a = jnp.exp2(m_old - m_new)
            l_ref[h] = a * l_ref[h] + jnp.sum(p, axis=0, keepdims=True)
            p_ref[h] = p.astype(BF16)
            acct_ref[h] = a * acct_ref[h]
            m_ref[h] = m_new
        for h in range(heads):
            acct_ref[h] += jnp.dot(vt_ref[h, j], p_ref[h], preferred_element_type=F32)

    def body(j, c):
        key_block(j, None)
        return c

    lax.fori_loop(0, qi, body, 0)
    key_idx = lax.broadcasted_iota(I32, (tq, tq), 0)
    qry_idx = lax.broadcasted_iota(I32, (tq, tq), 1)
    key_block(qi, key_idx <= qry_idx)
    for h in range(heads):
        sl = slice(h * HEAD_DIM, (h + 1) * HEAD_DIM)
        acc_ref[:, sl] = (acct_ref[h] / l_ref[h]).T
    y_ref[...] = _rms_gain(acc_ref[...], gn_ref[...]).astype(y_ref.dtype)


def _fox_prompt(q16, kv16, aug_q, aug_k, gn3, layer, n_seq, seq, tq):
    gw = q16.shape[1]
    heads = gw // HEAD_DIM
    nq = seq // tq
    qblk = pl.BlockSpec((tq, gw), lambda b, i: (b * nq + i, 0))
    est = 2 * (3 * tq * gw * 2 + 3 * seq * gw * 2) + seq * gw * 2 + tq * gw * 8 + 3 * heads * 8 * tq * 4 \
        + 8 * tq * tq * 4
    return pl.pallas_call(
        functools.partial(_fox_prompt_kernel, tq=tq, heads=heads, c2=HEAD_DIM ** -0.5 * math.log2(math.e)),
        out_shape=jax.ShapeDtypeStruct((n_seq * seq, gw), BF16),
        grid=(n_seq, nq),
        in_specs=[
            qblk, qblk,
            pl.BlockSpec((seq, gw), lambda b, i: (b, 0)),
            pl.BlockSpec((seq, gw), lambda b, i: (b, 0)),
            pl.BlockSpec((seq, gw), lambda b, i: (b, 1)),
            pl.BlockSpec((None, 1, gw), lambda b, i: (layer, 0, 0)),
        ],
        out_specs=qblk,
        scratch_shapes=[pltpu.VMEM((heads, nq, HEAD_DIM, tq), BF16),
                        pltpu.VMEM((heads, 1, tq), F32), pltpu.VMEM((heads, 1, tq), F32),
                        pltpu.VMEM((heads, HEAD_DIM, tq), F32), pltpu.VMEM((tq, gw), F32),
                        pltpu.VMEM((heads, tq, tq), F32), pltpu.VMEM((heads, tq, tq), BF16)],
        compiler_params=_params(est + heads * tq * tq * 6, ("arbitrary", "arbitrary")),
        name="fox_prompt",
    )(q16, aug_q, kv16, aug_k, kv16, gn3)


def _fox_sample_kernel(q_ref, kc_ref, vc_ref, kn_ref, vn_ref, cq_ref, ckc_ref, ckn_ref, gn_ref, y_ref,
                       m_ref, l_ref, acc_ref, *, ls, heads, scale, n_kc):
    kc = pl.program_id(1)

    @pl.when(kc == 0)
    def _():
        m_ref[...] = jnp.full(m_ref.shape, NEG_INF, F32)
        l_ref[...] = jnp.zeros(l_ref.shape, F32)
        acc_ref[...] = jnp.zeros(acc_ref.shape, F32)

    def update(h, k, v, ck, mask):
        sl = slice(h * HEAD_DIM, (h + 1) * HEAD_DIM)
        carry = (m_ref[:, h:h + 1], l_ref[:, h:h + 1], acc_ref[:, sl])
        m, l, acc = _softmax_step(q_ref[:, sl], k, v, cq_ref[:, h:h + 1], ck, carry, scale, mask)
        m_ref[:, h:h + 1] = m
        l_ref[:, h:h + 1] = l
        acc_ref[:, sl] = acc

    for h in range(heads):
        sl = slice(h * HEAD_DIM, (h + 1) * HEAD_DIM)
        update(h, kc_ref[:, sl].astype(BF16), vc_ref[:, sl].astype(BF16), ckc_ref[h, pl.ds(kc, 1), :], None)

    @pl.when(kc == n_kc - 1)
    def _():
        row = lax.broadcasted_iota(I32, (ls, ls), 0)
        col = lax.broadcasted_iota(I32, (ls, ls), 1)
        for h in range(heads):
            sl = slice(h * HEAD_DIM, (h + 1) * HEAD_DIM)
            update(h, kn_ref[:, sl], vn_ref[:, sl], ckn_ref[h:h + 1, :], col <= row)
        for h in range(heads):
            sl = slice(h * HEAD_DIM, (h + 1) * HEAD_DIM)
            acc_ref[:, sl] = acc_ref[:, sl] / l_ref[:, h:h + 1]
        y_ref[...] = _rms_gain(acc_ref[...], gn_ref[...]).astype(y_ref.dtype)


def _fox_sample(q16, kv16, cache_k, cache_v, cq, ckc4, ckn, gn3, layer, row0, n_seq, ls):
    gw = q16.shape[1]
    heads = gw // HEAD_DIM
    past = cache_k.shape[2]
    tkc = _pick(past, 512)
    n_kc = past // tkc
    blk0 = row0 // ls
    est = 2 * (3 * ls * gw * 2 + 2 * tkc * gw * 4 + heads * past * 4 + ls * gw * 2) + 3 * ls * gw * 4 \
        + 2 * tkc * gw * 2 + 6 * ls * tkc * 4
    return pl.pallas_call(
        functools.partial(_fox_sample_kernel, ls=ls, heads=heads, scale=HEAD_DIM ** -0.5, n_kc=n_kc),
        out_shape=jax.ShapeDtypeStruct((n_seq * ls, gw), BF16),
        grid=(n_seq, n_kc),
        in_specs=[
            pl.BlockSpec((ls, gw), lambda b, c: (blk0 + b, 0)),
            pl.BlockSpec((None, None, tkc, gw), lambda b, c: (layer, b, c, 0)),
            pl.BlockSpec((None, None, tkc, gw), lambda b, c: (layer, b, c, 0)),
            pl.BlockSpec((ls, gw), lambda b, c: (blk0 + b, 0)),
            pl.BlockSpec((ls, gw), lambda b, c: (blk0 + b, 1)),
            pl.BlockSpec((ls, heads), lambda b, c: (b, 0)),
            pl.BlockSpec((None, heads, n_kc, tkc), lambda b, c: (b, 0, 0, 0)),
            pl.BlockSpec((None, heads, ls), lambda b, c: (b, 0, 0)),
            pl.BlockSpec((None, 1, gw), lambda b, c: (layer, 0, 0)),
        ],
        out_specs=pl.BlockSpec((ls, gw), lambda b, c: (b, 0)),
        scratch_shapes=[pltpu.VMEM((ls, heads), F32), pltpu.VMEM((ls, heads), F32), pltpu.VMEM((ls, gw), F32)],
        compiler_params=_params(est, ("arbitrary", "arbitrary")),
        name="fox_sample",
    )(q16, cache_k, cache_v, kv16, kv16, cq, ckc4, ckn, gn3)


def _pool_kernel(*refs, tl, n_chunks, has_hist, pos0, hist_len):
    if has_hist:
        x_ref, hist_ref, wbd_ref, scale_ref, gn_ref, y_ref, state_ref, ext_ref = refs
    else:
        x_ref, wbd_ref, scale_ref, gn_ref, y_ref, state_ref, ext_ref = refs
    gw = x_ref.shape[1]
    pre = 16
    c = pl.program_id(1)

    @pl.when(c == 0)
    def _():
        ext_ref[0:pre, :] = jnp.zeros((pre, gw), F32)
        if has_hist:
            ext_ref[pre - hist_len:pre, :] = hist_ref[...]

    @pl.when(c > 0)
    def _():
        ext_ref[0:pre, :] = ext_ref[tl:tl + pre, :]

    x = x_ref[...].astype(F32)
    ext_ref[pre:pre + tl, :] = x
    pos = pos0 + c * tl + lax.broadcasted_iota(I32, (tl, 1), 0)
    pg = gw // len(POOL_WINDOWS)
    parts = []
    for g, w in enumerate(POOL_WINDOWS):
        cs = slice(g * pg, (g + 1) * pg)
        acc = x[:, cs]
        for j in range(1, w):
            acc = acc + ext_ref[pre - j:pre - j + tl, cs]
        cnt = jnp.minimum(w, pos + 1).astype(F32)
        parts.append(acc / cnt)
    pooled = jnp.concatenate(parts, axis=-1) - x
    mixed = jnp.dot(pooled.astype(BF16), wbd_ref[...], preferred_element_type=F32) * scale_ref[...]
    y_ref[...] = _rms_gain(mixed, gn_ref[...]).astype(y_ref.dtype)

    @pl.when(c == n_chunks - 1)
    def _():
        state_ref[...] = ext_ref[pre + tl - hist_len:pre + tl, :]


def _pool_mixer(z, col_blk, hist, wbd3, scale3, gn3, gn_blk, layer, row0, n_seq, seq, pos0):
    gw = wbd3.shape[-1]
    hist_len = max(POOL_WINDOWS) - 1
    tl = _pick(seq, 256)
    assert seq >= hist_len and tl >= 16 and row0 % tl == 0
    n_chunks = seq // tl
    blk0 = row0 // tl
    has_hist = hist is not None
    ins = [z] + ([hist] if has_hist else []) + [wbd3, scale3, gn3]
    specs = [pl.BlockSpec((tl, gw), lambda b, c: (blk0 + b * n_chunks + c, col_blk))]
    if has_hist:
        specs.append(pl.BlockSpec((None, None, hist_len, gw), lambda b, c: (layer, b, 0, 0)))
    specs += [
        pl.BlockSpec((None, gw, gw), lambda b, c: (layer, 0, 0)),
        pl.BlockSpec((None, 1, gw), lambda b, c: (layer, 0, 0)),
        pl.BlockSpec((None, 1, gw), lambda b, c: (layer, 0, gn_blk)),
    ]
    est = 2 * (tl * gw * 4 + gw * gw * 2 + tl * gw * 2 + 16 * gw * 4) + (tl + 16) * gw * 4 + 6 * tl * gw * 4
    return pl.pallas_call(
        functools.partial(_pool_kernel, tl=tl, n_chunks=n_chunks, has_hist=has_hist, pos0=pos0,
                          hist_len=hist_len),
        out_shape=(jax.ShapeDtypeStruct((n_seq * seq, gw), BF16),
                   jax.ShapeDtypeStruct((n_seq, hist_len, gw), F32)),
        grid=(n_seq, n_chunks),
        in_specs=specs,
        out_specs=(pl.BlockSpec((tl, gw), lambda b, c: (b * n_chunks + c, 0)),
                   pl.BlockSpec((None, hist_len, gw), lambda b, c: (b, 0, 0))),
        scratch_shapes=[pltpu.VMEM((tl + 16, gw), F32)],
        compiler_params=_params(est, ("arbitrary", "arbitrary")),
        name="pool_mixer",
    )(*ins)


def _conv_kernel(*refs, tl, n_chunks, has_hist, width):
    if has_hist:
        (a_ref, g_ref, hist_ref, wdw_ref, bdw_ref, lng_ref, lnb_ref, wpw_ref, bpw_ref, gn_ref,
         y_ref, state_ref, ext_ref, sh_ref) = refs
    else:
        (a_ref, g_ref, wdw_ref, bdw_ref, lng_ref, lnb_ref, wpw_ref, bpw_ref, gn_ref,
         y_ref, state_ref, ext_ref, sh_ref) = refs
    gw = a_ref.shape[1]
    hist_len = width - 1
    pre = 32
    c = pl.program_id(1)

    @pl.when(c == 0)
    def _():
        ext_ref[0:pre, :] = jnp.zeros((pre, gw), F32)
        if has_hist:
            ext_ref[pre - hist_len:pre, :] = hist_ref[...]

    @pl.when(c > 0)
    def _():
        ext_ref[0:pre, :] = ext_ref[tl:tl + pre, :]

    glu = a_ref[...].astype(F32) * jax.nn.sigmoid(g_ref[...].astype(F32))
    ext_ref[pre:pre + tl, :] = glu
    span = tl + pre - 8
    for r in range(1, 8):
        sh_ref[r - 1, 0:span, :] = ext_ref[r:r + span, :]
    acc = jnp.zeros((tl, gw), F32) + bdw_ref[...]
    base = pre - hist_len
    for j in range(width):
        r = (base + j) % 8
        a8 = base + j - r
        rows = ext_ref[a8:a8 + tl, :] if r == 0 else sh_ref[r - 1, a8:a8 + tl, :]
        acc = acc + rows * wdw_ref[j:j + 1, :]
    y = _layer_norm(acc, lng_ref[...], lnb_ref[...])
    y = y * jax.nn.sigmoid(y)
    out = jnp.dot(y.astype(BF16), wpw_ref[...], preferred_element_type=F32) + bpw_ref[...]
    y_ref[...] = _rms_gain(out, gn_ref[...]).astype(y_ref.dtype)

    @pl.when(c == n_chunks - 1)
    def _():
        state_ref[...] = ext_ref[pre + tl - hist_len:pre + tl, :]


def _conv_mixer(z, col_blk, hist, wdw, bdw3, lng3, lnb3, wpw3, bpw3, gn3, gn_blk, layer, row0, n_seq, seq):
    gw = wpw3.shape[-1]
    width = wdw.shape[1]
    hist_len = width - 1
    tl = _pick(seq, 256)
    assert seq >= hist_len and tl >= 32 and row0 % tl == 0
    n_chunks = seq // tl
    blk0 = row0 // tl
    has_hist = hist is not None
    vec = lambda blk=0: pl.BlockSpec((None, 1, gw), lambda b, c: (layer, 0, blk))
    ins = [z, z] + ([hist] if has_hist else []) + [wdw, bdw3, lng3, lnb3, wpw3, bpw3, gn3]
    specs = [pl.BlockSpec((tl, gw), lambda b, c: (blk0 + b * n_chunks + c, col_blk)),
             pl.BlockSpec((tl, gw), lambda b, c: (blk0 + b * n_chunks + c, col_blk + 1))]
    if has_hist:
        specs.append(pl.BlockSpec((None, None, hist_len, gw), lambda b, c: (layer, b, 0, 0)))
    specs += [pl.BlockSpec((None, width, gw), lambda b, c: (layer, 0, 0)), vec(), vec(), vec(),
              pl.BlockSpec((None, gw, gw), lambda b, c: (layer, 0, 0)), vec(), vec(gn_blk)]
    est = 2 * (2 * tl * gw * 4 + gw * gw * 2 + tl * gw * 2 + 64 * gw * 4) + 8 * (tl + 32) * gw * 4 + 8 * tl * gw * 4
    return pl.pallas_call(
        functools.partial(_conv_kernel, tl=tl, n_chunks=n_chunks, has_hist=has_hist, width=width),
        out_shape=(jax.ShapeDtypeStruct((n_seq * seq, gw), BF16),
                   jax.ShapeDtypeStruct((n_seq, hist_len, gw), F32)),
        grid=(n_seq, n_chunks),
        in_specs=specs,
        out_specs=(pl.BlockSpec((tl, gw), lambda b, c: (b * n_chunks + c, 0)),
                   pl.BlockSpec((None, hist_len, gw), lambda b, c: (b, 0, 0))),
        scratch_shapes=[pltpu.VMEM((tl + 32, gw), F32), pltpu.VMEM((7, tl + 32, gw), F32)],
        compiler_params=_params(est, ("arbitrary", "arbitrary")),
        name="conv_mixer",
    )(*ins)


def _gmlp_kernel(*refs, tl, lc, heads, emit_vn):
    if emit_vn:
        u_ref, v_ref, lng_ref, lnb_ref, ws_ref, bias_ref, gn_ref, y_ref, vn_ref = refs
    else:
        u_ref, v_ref, lng_ref, lnb_ref, ws_ref, bias_ref, gn_ref, y_ref = refs
    vn = _layer_norm(v_ref[...].astype(F32), lng_ref[...], lnb_ref[...])
    if emit_vn:
        vn_ref[...] = vn
    vb = vn.astype(BF16)
    rows = []
    for ch in range(tl // lc):
        rs = slice(ch * lc, (ch + 1) * lc)
        parts = [jnp.dot(ws_ref[h], vb[rs, h * HEAD_DIM:(h + 1) * HEAD_DIM], preferred_element_type=F32)
                 for h in range(heads)]
        rows.append(jnp.concatenate(parts, axis=-1) + bias_ref[...])
    mix = rows[0] if len(rows) == 1 else jnp.concatenate(rows, axis=0)
    y = u_ref[...].astype(F32) * mix
    y_ref[...] = _rms_gain(y, gn_ref[...]).astype(y_ref.dtype)


def _gmlp_mixer(z, col_blk, lng3, lnb3, ws4, bias3, gn3, gn_blk, layer, row0, n_rows, lc, emit_vn):
    gw = lng3.shape[-1]
    heads = gw // HEAD_DIM
    tl = lc * max(1, min(256 // lc, n_rows // lc))
    while n_rows % tl or row0 % tl:
        tl -= lc
    blk0 = row0 // tl
    vec = lambda blk=0: pl.BlockSpec((None, 1, gw), lambda i: (layer, 0, blk))
    out_shape = [jax.ShapeDtypeStruct((n_rows, gw), BF16)]
    out_specs = [pl.BlockSpec((tl, gw), lambda i: (i, 0))]
    if emit_vn:
        out_shape.append(jax.ShapeDtypeStruct((n_rows, gw), F32))
        out_specs.append(pl.BlockSpec((tl, gw), lambda i: (i, 0)))
    est = 2 * (2 * tl * gw * 4 + heads * lc * lc * 2 + lc * gw * 4 + tl * gw * 6) + 6 * tl * gw * 4
    outs = pl.pallas_call(
        functools.partial(_gmlp_kernel, tl=tl, lc=lc, heads=heads, emit_vn=emit_vn),
        out_shape=tuple(out_shape),
        grid=(n_rows // tl,),
        in_specs=[pl.BlockSpec((tl, gw), lambda i: (blk0 + i, col_blk)),
                  pl.BlockSpec((tl, gw), lambda i: (blk0 + i, col_blk + 1)),
                  vec(), vec(),
                  pl.BlockSpec((None, heads, lc, lc), lambda i: (layer, 0, 0, 0)),
                  pl.BlockSpec((None, lc, gw), lambda i: (layer, 0, 0)),
                  vec(gn_blk)],
        out_specs=tuple(out_specs),
        compiler_params=_params(est, ("arbitrary",)),
        name="gmlp_mixer",
    )(z, z, lng3, lnb3, ws4, bias3, gn3)
    return outs


def _memattn_kernel(q_ref, k_ref, v_ref, o_ref, *, heads, scale):
    parts = []
    for h in range(heads):
        sl = slice(h * HEAD_DIM, (h + 1) * HEAD_DIM)
        k = k_ref[:, sl].astype(BF16)
        v = v_ref[:, sl].astype(BF16)
        s = lax.dot_general(q_ref[:, sl], k, (((1,), (1,)), ((), ())), preferred_element_type=F32) * scale
        e = jnp.exp(s - jnp.max(s, axis=-1, keepdims=True))
        p = e / jnp.sum(e, axis=-1, keepdims=True)
        parts.append(jnp.dot(p.astype(BF16), v, preferred_element_type=F32))
    o_ref[...] = jnp.concatenate(parts, axis=-1).astype(o_ref.dtype)


def _mem_attention(q16, mem_k, mem_v, n_mem, row0, n_rows, rows_per_seq, kv_blk0):
    mw = q16.shape[1]
    heads = mw // HEAD_DIM
    tm = _pick(rows_per_seq, 256)
    per = rows_per_seq // tm
    blk0 = row0 // tm
    assert row0 % tm == 0
    est = 2 * (tm * mw * 2 + 2 * n_mem * mw * 4 + tm * mw * 2) + 8 * tm * n_mem * 4
    return pl.pallas_call(
        functools.partial(_memattn_kernel, heads=heads, scale=HEAD_DIM ** -0.5),
        out_shape=jax.ShapeDtypeStruct((n_rows, mw), BF16),
        grid=(n_rows // tm,),
        in_specs=[pl.BlockSpec((tm, mw), lambda i: (blk0 + i, 0)),
                  pl.BlockSpec((n_mem, mw), lambda i: (kv_blk0 + i // per, 0)),
                  pl.BlockSpec((n_mem, mw), lambda i: (kv_blk0 + i // per, 0))],
        out_specs=pl.BlockSpec((tm, mw), lambda i: (i, 0)),
        compiler_params=_params(est, ("arbitrary",)),
        name="mem_attention",
    )(q16, mem_k, mem_v)


def _router_kernel(x_ref, w_ref, b_ref, tril_ref, idx_ref, gate_ref, rank_ref, cnt_ref, carry_ref, *,
                   n_experts):
    @pl.when(pl.program_id(0) == 0)
    def _():
        carry_ref[...] = jnp.zeros(carry_ref.shape, F32)

    tm = x_ref.shape[0]
    logits = jnp.dot(x_ref[...], w_ref[...], preferred_element_type=F32) + b_ref[...]
    lane = lax.broadcasted_iota(I32, (tm, LANES), 1)
    lane_f = lane.astype(F32)
    work = jnp.where(lane < n_experts, logits, -jnp.inf)
    vals, sels = [], []
    idx_out = jnp.zeros((tm, LANES), I32)
    for k in range(TOP_K):
        m = jnp.max(work, axis=-1, keepdims=True)
        idx = jnp.min(jnp.where(work == m, lane_f, float(LANES)), axis=-1, keepdims=True)
        sel = lane_f == idx
        vals.append(m)
        sels.append(sel)
        idx_out = jnp.where(lane == k, idx.astype(I32), idx_out)
        work = jnp.where(sel, -jnp.inf, work)
    exps = [jnp.exp(v - vals[0]) for v in vals]
    denom = exps[0]
    for e in exps[1:]:
        denom = denom + e
    gate_out = jnp.zeros((tm, LANES), F32)
    for k in range(TOP_K):
        gate_out = jnp.where(lane == k, exps[k] / denom, gate_out)
    base = carry_ref[...]
    rank_out = jnp.zeros((tm, LANES), I32)
    tril = tril_ref[...]
    for k in range(TOP_K):
        onehot = jnp.where(sels[k], 1.0, 0.0)
        before = jnp.dot(tril, onehot.astype(BF16), preferred_element_type=F32) + base
        rank = jnp.sum(onehot * before, axis=-1, keepdims=True)
        rank_out = jnp.where(lane == k, rank.astype(I32), rank_out)
        base = base + jnp.sum(onehot, axis=0, keepdims=True)
    carry_ref[...] = base
    idx_ref[...] = idx_out
    gate_ref[...] = gate_out
    rank_ref[...] = rank_out
    cnt_ref[...] = base.astype(I32)


def _router(x16, w3, b3, layer, n_experts):
    m, d = x16.shape
    tm = _pick(m, 256, 16)
    tril = jnp.tri(tm, k=-1, dtype=BF16)
    row = pl.BlockSpec((tm, LANES), lambda i: (i, 0))
    est = 2 * (tm * d * 2 + d * LANES * 2 + tm * tm * 2 + 3 * tm * LANES * 4) + 16 * tm * LANES * 4
    return pl.pallas_call(
        functools.partial(_router_kernel, n_experts=n_experts),
        out_shape=(jax.ShapeDtypeStruct((m, LANES), I32), jax.ShapeDtypeStruct((m, LANES), F32),
                   jax.ShapeDtypeStruct((m, LANES), I32), jax.ShapeDtypeStruct((1, LANES), I32)),
        grid=(m // tm,),
        in_specs=[pl.BlockSpec((tm, d), lambda i: (i, 0)),
                  pl.BlockSpec((None, d, LANES), lambda i: (layer, 0, 0)),
                  pl.BlockSpec((None, 1, LANES), lambda i: (layer, 0, 0)),
                  pl.BlockSpec((tm, tm), lambda i: (0, 0))],
        out_specs=(row, row, row, pl.BlockSpec((1, LANES), lambda i: (0, 0))),
        scratch_shapes=[pltpu.VMEM((1, LANES), F32)],
        compiler_params=_params(est, ("arbitrary",)),
        name="router",
    )(x16, w3, b3, tril)


def _gather_rows(src_hbm, idx_smem, slot, buf, sem, n, unroll=8):
    def body(jj, _):
        base = pl.multiple_of(jj * unroll, unroll)
        for u in range(unroll):
            t = idx_smem[slot, base + u]
            pltpu.make_async_copy(_src_row(src_hbm, t, buf), buf.at[slot, pl.ds(base + u, 1)],
                                  sem.at[slot]).start(priority=u % 2)
        return 0
    lax.fori_loop(0, n // unroll, body, 0)


def _src_row(src_hbm, t, buf):
    return src_hbm.at[pl.ds(t, 1)] if len(buf.shape) == 4 else src_hbm.at[t]


def _wait_rows(src_hbm, slot, buf, sem, n, unroll=8):
    def body(j, _):
        pltpu.make_async_copy(_src_row(src_hbm, 0, buf), buf.at[slot, pl.ds(j, 1)], sem.at[slot]).wait()
        return 0
    lax.fori_loop(0, n, body, 0, unroll=unroll)


def _dispatch_kernel(tok_ref, x_hbm, o_ref, idx_smem, buf, tile_ref, isem, gsem, *, tmb, n_blk):
    rb = pl.program_id(0)
    slot = rb % 2
    nxt = 1 - slot

    def idx_copy(blk, s):
        return pltpu.make_async_copy(tok_ref.at[blk], idx_smem.at[s], isem.at[s])

    @pl.when(rb == 0)
    def _():
        idx_copy(0, 0).start()
        idx_copy(0, 0).wait()
        _gather_rows(x_hbm, idx_smem, 0, buf, gsem, tmb)
        if n_blk > 1:
            idx_copy(1, 1).start()

    @pl.when(rb + 1 < n_blk)
    def _():
        idx_copy(rb + 1, nxt).wait()
        _gather_rows(x_hbm, idx_smem, nxt, buf, gsem, tmb)

    @pl.when(rb + 2 < n_blk)
    def _():
        idx_copy(rb + 2, slot).start()

    _wait_rows(x_hbm, slot, buf, gsem, tmb)
    tile_ref[...] = buf[slot].reshape(tile_ref.shape)
    o_ref[...] = tile_ref[...].astype(o_ref.dtype)


def _dispatch(row_tok2, x_rows, tmb):
    n_blk = row_tok2.shape[0]
    d = x_rows.shape[-1]
    est = 2 * tmb * d * 4 + 2 * tmb * d * 2 + row_tok2.size * 4 * 2 + 2 * tmb * d * 4
    return pl.pallas_call(
        functools.partial(_dispatch_kernel, tmb=tmb, n_blk=n_blk),
        out_shape=jax.ShapeDtypeStruct((n_blk * tmb, d), BF16),
        grid=(n_blk,),
        in_specs=[pl.BlockSpec((n_blk, tmb), lambda i: (0, 0)),
                  pl.BlockSpec(memory_space=pl.ANY)],
        out_specs=pl.BlockSpec((tmb, d), lambda i: (i, 0)),
        scratch_shapes=[pltpu.SMEM((2, tmb), I32), pltpu.VMEM((2, tmb, 1, d), F32), pltpu.VMEM((tmb, d), F32),
                        pltpu.SemaphoreType.DMA((2,)), pltpu.SemaphoreType.DMA((2,))],
        compiler_params=_params(est, ("arbitrary",)),
        name="moe_dispatch",
    )(row_tok2, x_rows)


def _expert_changed(meta_ref, rb):
    prev = meta_ref[1 + jnp.maximum(rb - 1, 0)]
    return jnp.logical_or(rb == 0, meta_ref[1 + rb] != prev)


def _expert_up_kernel(meta_ref, x_ref, wg_ref, wu_ref, bg_ref, bu_ref, h_ref, wgb_ref, wub_ref):
    rb = pl.program_id(1)
    live = rb < meta_ref[0]

    @pl.when(jnp.logical_and(live, _expert_changed(meta_ref, rb)))
    def _():
        wgb_ref[...] = wg_ref[...].astype(BF16)
        wub_ref[...] = wu_ref[...].astype(BF16)

    @pl.when(live)
    def _():
        tmb = x_ref.shape[0]
        rc = _pick(tmb, 256)
        for r in range(tmb // rc):
            rs = slice(r * rc, (r + 1) * rc)
            x = x_ref[rs, :]
            gate = jnp.dot(x, wgb_ref[...], preferred_element_type=F32) + bg_ref[...]
            up = jnp.dot(x, wub_ref[...], preferred_element_type=F32) + bu_ref[...]
            gate = jnp.minimum(gate, SWIGLU_LIMIT)
            up = jnp.clip(up, -SWIGLU_LIMIT, SWIGLU_LIMIT)
            h_ref[rs, :] = (gate * jax.nn.sigmoid(SWIGLU_ALPHA * gate) * (up + 1.0)).astype(h_ref.dtype)

    @pl.when(jnp.logical_not(live))
    def _():
        h_ref[...] = jnp.zeros(h_ref.shape, h_ref.dtype)


def _expert_up(meta, xs, w_gate_up, b_gate_up4, layer, tmb):
    n_rows, d = xs.shape
    n_blk = n_rows // tmb
    ff = w_gate_up.shape[-1] // 2
    tf = _pick(ff, 512, LANES)
    nf = ff // tf
    live_blk = lambda rb, m: jnp.minimum(rb, m[0] - 1)
    w_spec = lambda off: pl.BlockSpec((None, None, d, tf), lambda f, rb, m: (layer, m[1 + rb], 0, off + f))
    b_spec = lambda off: pl.BlockSpec((None, None, 1, tf), lambda f, rb, m: (layer, m[1 + rb], 0, off + f))
    est = 2 * (tmb * d * 2 + 2 * d * tf * 4 + tmb * tf * 2) + 2 * d * tf * 2 + 6 * tmb * tf * 4
    return pl.pallas_call(
        _expert_up_kernel,
        out_shape=jax.ShapeDtypeStruct((n_rows, ff), BF16),
        grid_spec=pltpu.PrefetchScalarGridSpec(
            num_scalar_prefetch=1,
            grid=(nf, n_blk),
            in_specs=[pl.BlockSpec((tmb, d), lambda f, rb, m: (live_blk(rb, m), 0)),
                      w_spec(0), w_spec(nf), b_spec(0), b_spec(nf)],
            out_specs=pl.BlockSpec((tmb, tf), lambda f, rb, m: (rb, f)),
            scratch_shapes=[pltpu.VMEM((d, tf), BF16), pltpu.VMEM((d, tf), BF16)]),
        compiler_params=_params(est, ("arbitrary", "arbitrary")),
        name="expert_up",
    )(meta, xs, w_gate_up, w_gate_up, b_gate_up4, b_gate_up4)


def _expert_down_kernel(meta_ref, h_ref, w_ref, b_ref, y_ref, wb_ref):
    rb = pl.program_id(1)
    live = rb < meta_ref[0]

    @pl.when(jnp.logical_and(live, _expert_changed(meta_ref, rb)))
    def _():
        wb_ref[...] = w_ref[...].astype(BF16)

    @pl.when(live)
    def _():
        tmb, _, tn = y_ref.shape
        rc = _pick(tmb, 128)
        for r in range(tmb // rc):
            rs = slice(r * rc, (r + 1) * rc)
            y = jnp.dot(h_ref[rs, :], wb_ref[...], preferred_element_type=F32) + b_ref[...]
            y_ref[rs] = y.reshape(rc, 1, tn)

    @pl.when(jnp.logical_not(live))
    def _():
        y_ref[...] = jnp.zeros(y_ref.shape, y_ref.dtype)


def _expert_down(meta, hs, w_down, b_down4, layer, tmb):
    n_rows, ff = hs.shape
    n_blk = n_rows // tmb
    d = w_down.shape[-1]
    tn = _pick(d, 2048, LANES)
    live_blk = lambda rb, m: jnp.minimum(rb, m[0] - 1)
    est = 2 * (tmb * ff * 2 + ff * tn * 4 + tmb * tn * 4) + ff * tn * 2 + 2 * tmb * tn * 4
    return pl.pallas_call(
        _expert_down_kernel,
        out_shape=jax.ShapeDtypeStruct((n_rows, 1, d), F32),
        grid_spec=pltpu.PrefetchScalarGridSpec(
            num_scalar_prefetch=1,
            grid=(d // tn, n_blk),
            in_specs=[pl.BlockSpec((tmb, ff), lambda c, rb, m: (live_blk(rb, m), 0)),
                      pl.BlockSpec((None, None, ff, tn), lambda c, rb, m: (layer, m[1 + rb], 0, c)),
                      pl.BlockSpec((None, None, 1, tn), lambda c, rb, m: (layer, m[1 + rb], 0, c))],
            out_specs=pl.BlockSpec((tmb, 1, tn), lambda c, rb, m: (rb, 0, c)),
            scratch_shapes=[pltpu.VMEM((ff, tn), BF16)]),
        compiler_params=_params(est, ("arbitrary", "arbitrary")),
        name="expert_down",
    )(meta, hs, w_down, b_down4)


def _combine_kernel(dest_ref, y_hbm, gate_ref, h_ref, g_ref, b_ref, o32_ref, o16_ref,
                    idx_smem, buf, isem, gsem, *, tmc, n_blk, alpha):
    i = pl.program_id(0)
    slot = i % 2
    nxt = 1 - slot
    n = TOP_K * tmc

    def idx_copy(blk, s):
        return pltpu.make_async_copy(dest_ref.at[blk], idx_smem.at[s], isem.at[s])

    @pl.when(i == 0)
    def _():
        idx_copy(0, 0).start()
        idx_copy(0, 0).wait()
        _gather_rows(y_hbm, idx_smem, 0, buf, gsem, n)
        if n_blk > 1:
            idx_copy(1, 1).start()

    @pl.when(i + 1 < n_blk)
    def _():
        idx_copy(i + 1, nxt).wait()
        _gather_rows(y_hbm, idx_smem, nxt, buf, gsem, n)

    @pl.when(i + 2 < n_blk)
    def _():
        idx_copy(i + 2, slot).start()

    _wait_rows(y_hbm, slot, buf, gsem, n)
    gates = gate_ref[...]
    mix = None
    for k in range(TOP_K):
        part = gates[:, k:k + 1] * buf[slot, k * tmc:(k + 1) * tmc, :]
        mix = part if mix is None else mix + part
    out = _layer_norm(alpha * h_ref[...] + mix, g_ref[...], b_ref[...])
    o32_ref[...] = out
    o16_ref[...] = out.astype(BF16)


def _combine_ln(dest2, y, gates, h32, g3, b3, layer, alpha, tmc):
    m, d = h32.shape
    n_blk = m // tmc
    n = TOP_K * tmc
    row = pl.BlockSpec((tmc, d), lambda i: (i, 0))
    par = pl.BlockSpec((None, 1, d), lambda i: (layer, 0, 0))
    est = 2 * n * d * 4 + 2 * (tmc * d * 10 + tmc * LANES * 4) + dest2.size * 4 * 2 + 4 * tmc * d * 4
    return pl.pallas_call(
        functools.partial(_combine_kernel, tmc=tmc, n_blk=n_blk, alpha=alpha),
        out_shape=(jax.ShapeDtypeStruct((m, d), F32), jax.ShapeDtypeStruct((m, d), BF16)),
        grid=(n_blk,),
        in_specs=[pl.BlockSpec((n_blk, n), lambda i: (0, 0)),
                  pl.BlockSpec(memory_space=pl.ANY),
                  pl.BlockSpec((tmc, LANES), lambda i: (i, 0)),
                  row, par, par],
        out_specs=(row, row),
        scratch_shapes=[pltpu.SMEM((2, n), I32), pltpu.VMEM((2, n, d), F32),
                        pltpu.SemaphoreType.DMA((2,)), pltpu.SemaphoreType.DMA((2,))],
        compiler_params=_params(est, ("arbitrary",)),
        name="moe_combine_ln",
    )(dest2, y, gates, h32, g3, b3)


def _route_tables(idx, rank, counts, tmb, n_blk, tmc):
    t = idx.shape[0]
    n_exp = counts.shape[0]
    padded = (counts + tmb - 1) // tmb * tmb
    pad_end = jnp.cumsum(padded)
    pad_start = pad_end - padded
    onehot = idx[..., None] == jnp.arange(n_exp, dtype=I32)
    dest = rank + jnp.sum(jnp.where(onehot, pad_start, 0), axis=-1)
    n_used = pad_end[-1] // tmb
    blk = jnp.arange(n_blk, dtype=I32)
    blk_e = jnp.minimum(jnp.sum(pad_end[None, :] <= (blk * tmb)[:, None], axis=1), n_exp - 1).astype(I32)
    blk_e = jnp.where(blk < n_used, blk_e, blk_e[jnp.maximum(n_used - 1, 0)])
    meta = jnp.concatenate([n_used[None].astype(I32), blk_e])
    tok = jnp.broadcast_to(jnp.arange(t, dtype=I32)[:, None], dest.shape)
    row_tok = jnp.zeros((n_blk * tmb,), I32).at[dest.reshape(-1)].set(tok.reshape(-1), unique_indices=True)
    dest2 = dest.reshape(t // tmc, tmc, TOP_K).transpose(0, 2, 1).reshape(t // tmc, TOP_K * tmc)
    return meta, row_tok.reshape(n_blk, tmb), dest2


def kernel(x_prompt, x_sample, mem_prompt, cache_fox_k, cache_fox_v, cache_fox_logf, cache_pool, cache_conv,
           cache_mem_k, cache_mem_v, emb_ln_g, emb_ln_b, w_in, b_fgate, w_pool, pool_scale, w_dw, b_dw,
           conv_ln_g, conv_ln_b, w_pw, b_pw, gmlp_ln_g, gmlp_ln_b, w_spatial, b_spatial, group_norm_g, w_out,
           ln1_g, ln1_b, w_xq, w_xk, w_xv, w_xo, ln2_g, ln2_b, w_router, b_router, w_gate_up, b_gate_up,
           w_down, b_down, ln3_g, ln3_b):
    bp, seq, d = x_prompt.shape
    bs, ls, _ = x_sample.shape
    depth = w_in.shape[0]
    gw = d // 4
    heads = gw // HEAD_DIM
    past = cache_fox_k.shape[2]
    n_mem = mem_prompt.shape[1]
    mw = w_xq.shape[-1]
    n_exp = w_router.shape[-1]
    tp, ts = bp * seq, bs * ls
    t = tp + ts
    alpha = float((2 * depth) ** 0.25)
    lc_p = min(w_spatial.shape[-1], seq)
    lc_s = min(w_spatial.shape[-1], ls)
    tq = _pick(seq, 256)
    nq = seq // tq

    vec3 = lambda a: a.reshape(depth, 1, a.shape[-1])
    rest0 = 3 * gw + heads
    w_qkv16 = w_in[:, :, :3 * gw].astype(BF16)
    w_rest16 = w_in[:, :, rest0:].astype(BF16)
    w_out16 = w_out.astype(BF16)
    w_f = jnp.pad(w_in[:, :, 3 * gw:rest0], ((0, 0), (0, 0), (0, LANES - heads)))
    b_f = jnp.pad(b_fgate, ((0, 0), (0, LANES - heads))).reshape(depth, 1, LANES)
    n_win = len(POOL_WINDOWS)
    pg = gw // n_win
    wbd = jnp.zeros((depth, gw, gw), F32)
    for g in range(n_win):
        wbd = wbd.at[:, g * pg:(g + 1) * pg, g * pg:(g + 1) * pg].set(w_pool[:, g])
    wbd16 = wbd.astype(BF16)
    wpw16 = w_pw.astype(BF16)

    def spatial(lc):
        causal = jnp.tril(jnp.ones((lc, lc), dtype=bool))
        ws = jnp.where(causal[None, None], w_spatial[:, :, :lc, :lc], 0).astype(BF16)
        bias = jnp.repeat(jnp.transpose(b_spatial[:, :, :lc], (0, 2, 1)), HEAD_DIM, axis=-1)
        return ws, bias

    ws_p, bias_p = spatial(lc_p)
    ws_s, bias_s = spatial(lc_s)
    w_router16 = jnp.pad(w_router, ((0, 0), (0, 0), (0, LANES - n_exp))).astype(BF16)
    b_router3 = jnp.pad(b_router, ((0, 0), (0, LANES - n_exp))).reshape(depth, 1, LANES)
    b_gu4 = b_gate_up.reshape(depth, n_exp, 1, b_gate_up.shape[-1])
    b_dn4 = b_down.reshape(depth, n_exp, 1, d)
    gn3 = vec3(group_norm_g)
    cache_k4 = cache_fox_k.reshape(depth, bs, past, gw)
    cache_v4 = cache_fox_v.reshape(depth, bs, past, gw)
    cache_mk = cache_mem_k.reshape(depth * bs * n_mem, mw)
    cache_mv = cache_mem_v.reshape(depth * bs * n_mem, mw)
    mem16 = mem_prompt.reshape(bp * n_mem, d).astype(BF16)

    tmb = 512 if t >= 4096 else 64
    n_blk = -(-(t * TOP_K + n_exp * (tmb - 1)) // tmb)
    tmc = _pick(t, 64)

    x_all = jnp.concatenate([x_prompt.reshape(tp, d), x_sample.reshape(ts, d)], axis=0)
    x32, x16 = _add_ln(x_all, None, emb_ln_g.reshape(1, 1, d), emb_ln_b.reshape(1, 1, d), 0, alpha)

    outs = [[] for _ in range(13)]
    for l in range(depth):
        (q16,) = _dense_mm([x16], w_qkv16, l, 0, gw, (BF16,), tn_target=1024)
        kv32, kv16 = _dense_mm([x16], w_qkv16, l, gw, 2 * gw, (F32, BF16), tn_target=1024)
        (zr,) = _dense_mm([x16], w_rest16, l, 0, 5 * gw, (BF16,), tn_target=1024)
        (logf_pad,) = _dense_mm([x16], w_f, l, 0, LANES, (F32,), bias3=b_f, epilogue="log_sigmoid")
        logf = logf_pad[:, :heads]

        lf_p = logf[:tp].reshape(bp, seq, heads)
        lf_s = logf[tp:].reshape(bs, ls, heads)
        cum_parts = _cumsum_time(jnp.transpose(lf_p, (1, 0, 2)).reshape(seq, bp * heads),
                                 split_scale=HEAD_DIM ** 0.5)
        hi, mid, lo = (jnp.transpose(c.reshape(seq, bp, heads), (1, 0, 2)).astype(F32)[..., None]
                       for c in cum_parts[1:])
        lane = jnp.arange(HEAD_DIM)
        place = lambda a, b, c, off: (jnp.where(lane == off, a, 0.0) + jnp.where(lane == off + 1, b, 0.0)
                                      + jnp.where(lane == off + 2, c, 0.0))
        ones_at = lambda off: ((lane >= off) & (lane < off + 3)).astype(F32)
        aug_q = (place(hi, mid, lo, 0) + ones_at(3)).astype(BF16).reshape(tp, gw)
        aug_k = (ones_at(0) - place(hi, mid, lo, 3)).astype(BF16).reshape(tp, gw)
        lk = past + ls
        lkp = -(-lk // 256) * 256
        lf_all = jnp.concatenate([cache_fox_logf[l].astype(F32), lf_s], axis=1)
        lf_all = jnp.pad(jnp.transpose(lf_all, (1, 0, 2)).reshape(lk, bs * heads), ((0, lkp - lk), (0, 0)))
        cum_s = jnp.transpose(_cumsum_time(lf_all)[0][:lk].reshape(lk, bs, heads), (1, 0, 2))
        cq_s = cum_s[:, past:].reshape(ts, heads)
        ck_s = jnp.transpose(cum_s, (0, 2, 1))
        tkc = _pick(past, 512)
        ckc = ck_s[:, :, :past].reshape(bs, heads, past // tkc, tkc)
        ckn = ck_s[:, :, past:]

        ya_p = _fox_prompt(q16, kv16, aug_q, aug_k, gn3, l, bp, seq, tq)
        ya_s = _fox_sample(q16, kv16, cache_k4, cache_v4, cq_s, ckc, ckn, gn3, l, tp, bs, ls)
        yb_p, pool_p = _pool_mixer(zr, 0, None, wbd16, vec3(pool_scale), gn3, 1, l, 0, bp, seq, 0)
        yb_s, pool_s = _pool_mixer(zr, 0, cache_pool, wbd16, vec3(pool_scale), gn3, 1, l, tp, bs, ls, past)
        conv_args = (w_dw, vec3(b_dw), vec3(conv_ln_g), vec3(conv_ln_b), wpw16, vec3(b_pw), gn3, 2, l)
        yc_p, conv_p = _conv_mixer(zr, 1, None, *conv_args, 0, bp, seq)
        yc_s, conv_s = _conv_mixer(zr, 1, cache_conv, *conv_args, tp, bs, ls)
        (yd_p,) = _gmlp_mixer(zr, 3, vec3(gmlp_ln_g), vec3(gmlp_ln_b), ws_p, bias_p, gn3, 3, l, 0, tp, lc_p, False)
        yd_s, gv_s = _gmlp_mixer(zr, 3, vec3(gmlp_ln_g), vec3(gmlp_ln_b), ws_s, bias_s, gn3, 3, l, tp, ts, lc_s, True)
        (mix,) = _dense_mm([(ya_p, ya_s), (yb_p, yb_s), (yc_p, yc_s), (yd_p, yd_s)], w_out16, l, 0, d, (F32,),
                           tn_target=1024)
        h1_32, h1_16 = _add_ln(x32, mix, vec3(ln1_g), vec3(ln1_b), l, alpha)
        (qm16,) = _dense_mm([h1_16], w_xq, l, 0, mw, (BF16,))
        (mk,) = _dense_mm([mem16], w_xk, l, 0, mw, (F32,))
        (mv,) = _dense_mm([mem16], w_xv, l, 0, mw, (F32,))
        om_p = _mem_attention(qm16, mk, mv, n_mem, 0, tp, seq, 0)
        om_s = _mem_attention(qm16, cache_mk, cache_mv, n_mem, tp, ts, ls, l * bs)
        (xo,) = _dense_mm([(om_p, om_s)], w_xo, l, 0, d, (F32,), tn_target=1024)
        h2_32, h2_16, h2_rows = _add_ln(h1_32, xo, vec3(ln2_g), vec3(ln2_b), l, alpha, emit_rows=True)

        idx_pad, gates, rank_pad, cnt_pad = _router(h2_16, w_router16, b_router3, l, n_exp)
        meta, row_tok2, dest2 = _route_tables(idx_pad[:, :TOP_K], rank_pad[:, :TOP_K], cnt_pad[0, :n_exp],
                                              tmb, n_blk, tmc)
        xs = _dispatch(row_tok2, h2_rows, tmb)
        hs = _expert_up(meta, xs, w_gate_up, b_gu4, l, tmb)
        y = _expert_down(meta, hs, w_down, b_dn4, l, tmb)
        x32, x16 = _combine_ln(dest2, y, gates, h2_32, vec3(ln3_g), vec3(ln3_b), l, alpha, tmc)

        k32, v32 = kv32[:, :gw], kv32[:, gw:]
        per_layer = (
            k32[:tp].reshape(bp, seq, heads, HEAD_DIM), v32[:tp].reshape(bp, seq, heads, HEAD_DIM), lf_p,
            pool_p, conv_p, mk.reshape(bp, n_mem, mw // HEAD_DIM, HEAD_DIM),
            mv.reshape(bp, n_mem, mw // HEAD_DIM, HEAD_DIM),
            k32[tp:].reshape(bs, ls, heads, HEAD_DIM), v32[tp:].reshape(bs, ls, heads, HEAD_DIM), lf_s,
            pool_s, conv_s, gv_s.reshape(bs, ls, heads, HEAD_DIM))
        for acc, val in zip(outs, per_layer):
            acc.append(val)

    return (x32[:tp].reshape(bp, seq, d), x32[tp:].reshape(bs, ls, d)) + tuple(jnp.stack(o) for o in outs)
```
